```python
import functools
import jax
import jax.numpy as jnp
from jax import lax
import numpy as np

D_MODEL = 1024
BATCH = 32
SEQ = 2048
DEPTH = 4
DEC_BATCH = 8
DEC_SEQ = 32
PAST_LEN = 1024

CHUNK = 64
WINDOW = 128
WIN_CHUNKS = WINDOW // CHUNK
A_HEADS = 8
A_KV_HEADS = 2
A_HEAD_DIM = 64
A_GROUP = A_HEADS // A_KV_HEADS
A_SCALE = A_HEAD_DIM ** -0.5
ROT_DIM = A_HEAD_DIM // 4
ROPE_THETA = 500000.0
B_HEADS = 4
B_DK = (D_MODEL // 2) // B_HEADS
B_DV = D_MODEL // B_HEADS
GATE_RANK = 16
GATE_TAU = 16.0
N_MEM = 256
M_HEADS = 4
M_HEAD_DIM = D_MODEL // 8
D_FF = 4 * D_MODEL
N_BRANCH = 3
DN_ALPHA = (2 * DEPTH) ** 0.25
DN_BETA = (8 * DEPTH) ** -0.25
LN_EPS = 1e-5
RMS_EPS = 1e-6
NEG_INF = -1e30

A_Q = A_HEADS * A_HEAD_DIM
A_KV = A_KV_HEADS * A_HEAD_DIM
B_QK = B_HEADS * B_DK
B_V = B_HEADS * B_DV
M_Q = M_HEADS * M_HEAD_DIM
SPLITS = (A_Q, A_KV, A_KV, B_QK, B_QK, B_V, B_V, GATE_RANK, M_Q, N_BRANCH * D_MODEL)
D_IN = A_Q + 2 * A_KV + 2 * B_QK + 2 * B_V + GATE_RANK + M_Q + N_BRANCH * D_MODEL

kernel_name = 'hybrid_streaming_encoder_step'


def split_columns(h):
    parts, off = [], 0
    for width in SPLITS:
        parts.append(h[..., off:off + width])
        off += width
    return parts


def layer_norm(x, g, b):
    xf = x.astype(jnp.float32)
    mu = jnp.mean(xf, axis=-1, keepdims=True)
    var = jnp.mean(jnp.square(xf - mu), axis=-1, keepdims=True)
    return ((xf - mu) * lax.rsqrt(var + LN_EPS) * g.astype(jnp.float32) + b.astype(jnp.float32)).astype(x.dtype)


def rms_norm(x, g):
    xf = x.astype(jnp.float32)
    return xf * lax.rsqrt(jnp.mean(xf * xf, axis=-1, keepdims=True) + RMS_EPS) * g.astype(jnp.float32)


def rope(x, pos):
    half = ROT_DIM // 2
    inv = ROPE_THETA ** (-jnp.arange(half, dtype=jnp.float32) / half)
    ang = pos.astype(jnp.float32)[:, None] * inv[None, :]
    cos = jnp.cos(ang)[:, None, :].astype(x.dtype)
    sin = jnp.sin(ang)[:, None, :].astype(x.dtype)
    x1 = x[..., :half]
    x2 = x[..., half:ROT_DIM]
    return jnp.concatenate([x1 * cos - x2 * sin, x2 * cos + x1 * sin, x[..., ROT_DIM:]], axis=-1)


def sink_softmax(s, sink):
    sk = sink.astype(jnp.float32).reshape(A_KV_HEADS, A_GROUP)[:, :, None, None]
    m = jnp.maximum(jnp.max(s, axis=-1, keepdims=True), sk)
    p = jnp.exp(s - m)
    return p / (jnp.sum(p, axis=-1, keepdims=True) + jnp.exp(sk - m))


def window_attn_prompt(q, k, v, sink):
    B, S = q.shape[0], q.shape[1]
    n = S // CHUNK
    J = (WIN_CHUNKS + 1) * CHUNK
    qb = q.reshape(B, n, CHUNK, A_KV_HEADS, A_GROUP, A_HEAD_DIM)
    pad = ((0, 0), (WIN_CHUNKS, 0), (0, 0), (0, 0), (0, 0))
    kb = jnp.pad(k.reshape(B, n, CHUNK, A_KV_HEADS, A_HEAD_DIM), pad)
    vb = jnp.pad(v.reshape(B, n, CHUNK, A_KV_HEADS, A_HEAD_DIM), pad)
    kw = jnp.concatenate([kb[:, j:j + n] for j in range(WIN_CHUNKS + 1)], axis=2)
    vw = jnp.concatenate([vb[:, j:j + n] for j in range(WIN_CHUNKS + 1)], axis=2)
    s = jnp.einsum('bnqkgd,bnjkd->bnkgqj', qb, kw).astype(jnp.float32) * A_SCALE
    key_chunk = jnp.arange(n)[:, None] - WIN_CHUNKS + jnp.arange(J)[None, :] // CHUNK
    s = jnp.where((key_chunk >= 0)[None, :, None, None, None, :], s, NEG_INF)
    p = sink_softmax(s, sink).astype(v.dtype)
    o = jnp.einsum('bnkgqj,bnjkd->bnqkgd', p, vw).reshape(B, S, A_Q)
    keep = min(WINDOW, S)
    return o, k[:, S - keep:], v[:, S - keep:]


def window_attn_sample(q, k, v, sink, k_cache, v_cache):
    B, T = q.shape[0], q.shape[1]
    kk = jnp.concatenate([k_cache.astype(k.dtype), k], axis=1)
    vv = jnp.concatenate([v_cache.astype(v.dtype), v], axis=1)
    qg = q.reshape(B, T, A_KV_HEADS, A_GROUP, A_HEAD_DIM)
    s = jnp.einsum('btkgd,bjkd->bkgtj', qg, kk).astype(jnp.float32) * A_SCALE
    p = sink_softmax(s, sink).astype(vv.dtype)
    o = jnp.einsum('bkgtj,bjkd->btkgd', p, vv).reshape(B, T, A_Q)
    keep = k_cache.shape[1]
    return o, kk[:, -keep:], vv[:, -keep:]


def gla_scan(q, k, v, log_a, s0, block):
    f32 = jnp.float32
    B, S, H = q.shape[0], q.shape[1], q.shape[2]
    n = S // block
    qf = q.astype(f32).reshape(B, n, block, H, B_DK) * (B_DK ** -0.5)
    kf = k.astype(f32).reshape(B, n, block, H, B_DK)
    vf = v.astype(f32).reshape(B, n, block, H, B_DV)
    cum = jnp.cumsum(log_a.reshape(B, n, block, H, B_DK), axis=2)
    last = cum[:, :, -1]
    q_dec = qf * jnp.exp(cum)
    k_inv = kf * jnp.exp(-cum)
    k_end = kf * jnp.exp(last[:, :, None] - cum)
    causal = jnp.tril(jnp.ones((block, block), f32))
    att = jnp.einsum('bnthd,bnshd->bnhts', q_dec, k_inv) * causal
    o_intra = jnp.einsum('bnhts,bnshv->bnthv', att, vf)

    def step(state, xs):
        qd, ke, vb, lb = xs
        o = jnp.einsum('bthd,bhdv->bthv', qd, state)
        state = jnp.exp(lb)[..., None] * state + jnp.einsum('bthd,bthv->bhdv', ke, vb)
        return state, o

    xs = (jnp.moveaxis(q_dec, 1, 0), jnp.moveaxis(k_end, 1, 0), jnp.moveaxis(vf, 1, 0), jnp.moveaxis(last, 1, 0))
    s_fin, o_inter = lax.scan(step, s0.astype(f32), xs)
    o = o_intra + jnp.moveaxis(o_inter, 0, 1)
    return o.reshape(B, S, H, B_DV), s_fin


def memory_attn(q, mem_k, mem_v):
    B, S = q.shape[0], q.shape[1]
    s = jnp.einsum('bshd,bmhd->bhsm', q, mem_k.astype(q.dtype)).astype(jnp.float32) * (M_HEAD_DIM ** -0.5)
    p = jax.nn.softmax(s, axis=-1).astype(q.dtype)
    return jnp.einsum('bhsm,bmhd->bshd', p, mem_v.astype(q.dtype)).reshape(B, S, M_Q)


def setup_inputs(seed: int = 0) -> dict:
    key = jax.random.key(seed)
    ks = jax.random.split(key, 32)
    f32 = jnp.float32

    def nrm(k, shape, scale):
        return jax.random.normal(k, shape, f32) * scale

    win_keep = min(WINDOW, PAST_LEN)
    return {
        'x_prompt': nrm(ks[0], (BATCH, SEQ, D_MODEL), 1.0),
        'x_sample': nrm(ks[1], (DEC_BATCH, DEC_SEQ, D_MODEL), 1.0),
        'cache_win_k': nrm(ks[2], (DEPTH, DEC_BATCH, win_keep, A_KV_HEADS, A_HEAD_DIM), 1.0),
        'cache_win_v': nrm(ks[3], (DEPTH, DEC_BATCH, win_keep, A_KV_HEADS, A_HEAD_DIM), 1.0),
        'state_gla': nrm(ks[4], (DEPTH, DEC_BATCH, B_HEADS, B_DK, B_DV), 1.0),
        'cache_mem_k': nrm(ks[5], (DEPTH, DEC_BATCH, N_MEM, M_HEADS, M_HEAD_DIM), 1.0),
        'cache_mem_v': nrm(ks[6], (DEPTH, DEC_BATCH, N_MEM, M_HEADS, M_HEAD_DIM), 1.0),
        'mem_prompt': nrm(ks[7], (BATCH, N_MEM, D_MODEL), 1.0),
        'w_in': nrm(ks[8], (DEPTH, D_MODEL, D_IN), D_MODEL ** -0.5),
        'w_gk2': nrm(ks[9], (DEPTH, GATE_RANK, B_QK), GATE_RANK ** -0.5),
        'b_gk': nrm(ks[10], (DEPTH, B_QK), 0.1),
        'attn_sinks': nrm(ks[11], (DEPTH, A_HEADS), 0.5),
        'gla_norm_g': 1.0 + nrm(ks[12], (DEPTH, B_DV), 0.02),
        'w_mem_kv': nrm(ks[13], (DEPTH, D_MODEL, 2 * M_Q), D_MODEL ** -0.5),
        'w_proj_a': nrm(ks[14], (DEPTH, A_Q, D_MODEL), A_Q ** -0.5),
        'w_proj_b': nrm(ks[15], (DEPTH, B_V, D_MODEL), B_V ** -0.5),
        'w_proj_m': nrm(ks[16], (DEPTH, M_Q, D_MODEL), M_Q ** -0.5),
        'w_out': nrm(ks[17], (DEPTH, D_MODEL, D_MODEL), D_MODEL ** -0.5 * DN_BETA),
        'ln1_g': 1.0 + nrm(ks[18], (DEPTH, D_MODEL), 0.02),
        'ln1_b': nrm(ks[19], (DEPTH, D_MODEL), 0.01),
        'w_up': nrm(ks[20], (DEPTH, D_MODEL, D_FF), D_MODEL ** -0.5),
        'b_up': nrm(ks[21], (DEPTH, D_FF), 0.01),
        'w_down': nrm(ks[22], (DEPTH, D_FF, D_MODEL), D_FF ** -0.5 * DN_BETA),
        'b_down': nrm(ks[23], (DEPTH, D_MODEL), 0.01),
        'ln2_g': 1.0 + nrm(ks[24], (DEPTH, D_MODEL), 0.02),
        'ln2_b': nrm(ks[25], (DEPTH, D_MODEL), 0.01),
    }


def reference(x_prompt, x_sample, cache_win_k, cache_win_v, state_gla, cache_mem_k, cache_mem_v, mem_prompt,
              w_in, w_gk2, b_gk, attn_sinks, gla_norm_g, w_mem_kv, w_proj_a, w_proj_b, w_proj_m, w_out,
              ln1_g, ln1_b, w_up, b_up, w_down, b_down, ln2_g, ln2_b):
    f32 = jnp.float32

    def layer(x, pos, l, window_fn, gla_s0, gla_block, mem_k, mem_v):
        B, S = x.shape[0], x.shape[1]
        aq, ak, av, bq, bk, bv, bg, bgk, mq, gates = split_columns(x @ w_in[l])
        aq = rope(aq.reshape(B, S, A_HEADS, A_HEAD_DIM), pos)
        ak = rope(ak.reshape(B, S, A_KV_HEADS, A_HEAD_DIM), pos)
        o_a, k_keep, v_keep = window_fn(aq, ak, av.reshape(B, S, A_KV_HEADS, A_HEAD_DIM), attn_sinks[l])
        log_a = jax.nn.log_sigmoid((bgk @ w_gk2[l] + b_gk[l]).astype(f32)) / GATE_TAU
        o_b, s_new = gla_scan(bq.reshape(B, S, B_HEADS, B_DK), bk.reshape(B, S, B_HEADS, B_DK),
                              bv.reshape(B, S, B_HEADS, B_DV), log_a.reshape(B, S, B_HEADS, B_DK),
                              gla_s0, gla_block)
        o_b = (rms_norm(o_b, gla_norm_g[l]) * jax.nn.silu(bg.reshape(B, S, B_HEADS, B_DV).astype(f32))).astype(x.dtype)
        o_m = memory_attn(mq.reshape(B, S, M_HEADS, M_HEAD_DIM), mem_k, mem_v)
        g_a, g_b, g_m = jnp.split(jax.nn.sigmoid(gates), N_BRANCH, axis=-1)
        merged = (g_a * (o_a @ w_proj_a[l])
                  + g_b * (o_b.reshape(B, S, B_V) @ w_proj_b[l])
                  + g_m * (o_m @ w_proj_m[l]))
        x = layer_norm(DN_ALPHA * x + merged @ w_out[l], ln1_g[l], ln1_b[l])
        ff = jnp.square(jax.nn.relu(x @ w_up[l] + b_up[l])) @ w_down[l] + b_down[l]
        x = layer_norm(DN_ALPHA * x + ff, ln2_g[l], ln2_b[l])
        return x, k_keep, v_keep, s_new.astype(x.dtype)

    Bp, Sp = x_prompt.shape[0], x_prompt.shape[1]
    pos_p = jnp.arange(Sp, dtype=jnp.int32)
    x = x_prompt
    wk_p, wv_p, gs_p, mk_p, mv_p = [], [], [], [], []
    for l in range(DEPTH):
        mkv = (mem_prompt.astype(x.dtype) @ w_mem_kv[l]).reshape(Bp, N_MEM, 2, M_HEADS, M_HEAD_DIM)
        mk, mv = mkv[:, :, 0], mkv[:, :, 1]
        s0 = jnp.zeros((Bp, B_HEADS, B_DK, B_DV), f32)
        x, kk, vv, st = layer(x, pos_p, l, window_attn_prompt, s0, CHUNK, mk, mv)
        wk_p.append(kk)
        wv_p.append(vv)
        gs_p.append(st)
        mk_p.append(mk)
        mv_p.append(mv)
    y_prompt = x

    T = x_sample.shape[1]
    pos_s = PAST_LEN + jnp.arange(T, dtype=jnp.int32)
    x = x_sample
    wk_s, wv_s, gs_s = [], [], []
    for l in range(DEPTH):
        fn = functools.partial(window_attn_sample, k_cache=cache_win_k[l], v_cache=cache_win_v[l])
        x, kk, vv, st = layer(x, pos_s, l, fn, state_gla[l], T, cache_mem_k[l], cache_mem_v[l])
        wk_s.append(kk)
        wv_s.append(vv)
        gs_s.append(st)
    y_sample = x

    return (y_prompt, y_sample, jnp.stack(wk_p), jnp.stack(wv_p), jnp.stack(gs_p), jnp.stack(mk_p),
            jnp.stack(mv_p), jnp.stack(wk_s), jnp.stack(wv_s), jnp.stack(gs_s))
```

```python
import functools

import jax
import jax.numpy as jnp
from jax import lax
from jax.experimental import pallas as pl
from jax.experimental.pallas import tpu as pltpu

D_MODEL = 1024
PAST_LEN = 1024
CHUNK = 64
WINDOW = 128
A_HEADS = 8
A_KV_HEADS = 2
A_HEAD_DIM = 64
A_GROUP = A_HEADS // A_KV_HEADS
A_SCALE = A_HEAD_DIM ** -0.5
ROT_DIM = A_HEAD_DIM // 4
ROPE_THETA = 500000.0
A_Q = A_HEADS * A_HEAD_DIM
A_KV = A_KV_HEADS * A_HEAD_DIM
B_HEADS = 4
B_DK = 128
B_DV = 256
B_QK = B_HEADS * B_DK
B_V = B_HEADS * B_DV
GATE_RANK = 16
GATE_TAU = 16.0
M_HEADS = 4
M_HEAD_DIM = 128
M_Q = M_HEADS * M_HEAD_DIM
D_FF = 4 * D_MODEL
N_BRANCH = 3
LN_EPS = 1e-5
RMS_EPS = 1e-6
NEG_INF = -1e30

LANES = 128
GK_PAD = LANES
VMEM_LIMIT_BYTES = 60000 * 1024

_C_AQ = 0
_C_AK = _C_AQ + A_Q
_C_AV = _C_AK + A_KV
_C_BQ = _C_AV + A_KV
_C_BK = _C_BQ + B_QK
_C_BV = _C_BK + B_QK
_C_BG = _C_BV + B_V
_C_GK = _C_BG + B_V
_C_MQ = _C_GK + GK_PAD
_C_GT = _C_MQ + M_Q
_C_END = _C_GT + N_BRANCH * D_MODEL

_NT = (((1,), (1,)), ((), ()))
_TN = (((0,), (0,)), ((), ()))

bf16 = jnp.bfloat16
f32 = jnp.float32


def _dot(a, b):
    return jnp.dot(a, b, preferred_element_type=f32)


def _dot_nt(a, b):
    return lax.dot_general(a, b, _NT, preferred_element_type=f32)


def _dot_tn(a, b):
    return lax.dot_general(a, b, _TN, preferred_element_type=f32)


def _layer_norm(z, g, b):
    mu = jnp.mean(z, axis=-1, keepdims=True)
    zc = z - mu
    var = jnp.mean(zc * zc, axis=-1, keepdims=True)
    return zc * lax.rsqrt(var + LN_EPS) * g + b


def _sigmoid(z):
    return 1.0 / (1.0 + jnp.exp(-z))


def _mixer_kernel(*refs, tile, chunk, keep, has_cache, mask_start, alpha):
    it = iter(refs)
    sinks_ref = next(it)
    x_ref = next(it)
    cos_ref, sin_lo_ref, sin_hi_ref = next(it), next(it), next(it)
    mk_ref, mv_ref = next(it), next(it)
    if has_cache:
        ck_ref, cv_ref, s0_ref = next(it), next(it), next(it)
    w_in_ref, w_gk2_ref, b_gk_ref, gn_ref = next(it), next(it), next(it), next(it)
    wpa_ref, wpb_ref, wpm_ref, wout_ref = next(it), next(it), next(it), next(it)
    ln_g_ref, ln_b_ref = next(it), next(it)
    y_ref, wk_ref, wv_ref, st_ref = next(it), next(it), next(it), next(it)
    klo_s, khi_s, v_s, st_s = next(it), next(it), next(it), next(it)

    t = pl.program_id(1)
    last_t = pl.num_programs(1) - 1
    n_ch = tile // chunk
    win = keep + chunk

    lane = lax.broadcasted_iota(jnp.int32, (1, LANES), 1)
    low_half = lane < A_HEAD_DIM

    @pl.when(t == 0)
    def _init():
        if has_cache:
            ck = ck_ref[0]
            klo_s[0:keep, :] = jnp.where(low_half, ck, 0.0).astype(bf16)
            khi_s[0:keep, :] = jnp.where(low_half, 0.0, ck).astype(bf16)
            v_s[0:keep, :] = cv_ref[0].astype(bf16)
            for h in range(B_HEADS):
                st_s[h] = s0_ref[0, h].T
        else:
            zero = jnp.zeros((keep, LANES), bf16)
            klo_s[0:keep, :] = zero
            khi_s[0:keep, :] = zero
            v_s[0:keep, :] = zero
            st_s[...] = jnp.zeros(st_s.shape, f32)

    x = x_ref[0]
    xb = x.astype(bf16)

    cos = cos_ref[...]
    sin_lo = sin_lo_ref[...]
    sin_hi = sin_hi_ref[...]

    def rope(z):
        return z * cos + pltpu.roll(z, LANES - ROT_DIM // 2, 1) * sin_lo + pltpu.roll(z, ROT_DIM // 2, 1) * sin_hi

    qkv = _dot(xb, w_in_ref[:, _C_AQ:_C_BQ])
    q_cols = [(rope(qkv[:, j * LANES:(j + 1) * LANES]) * A_SCALE).astype(bf16) for j in range(A_GROUP)]
    k_r = rope(qkv[:, _C_AK:_C_AV])
    v_new = qkv[:, _C_AV:_C_BQ]
    klo_s[keep:keep + tile, :] = jnp.where(low_half, k_r, 0.0).astype(bf16)
    khi_s[keep:keep + tile, :] = jnp.where(low_half, 0.0, k_r).astype(bf16)
    v_s[keep:keep + tile, :] = v_new.astype(bf16)

    @pl.when(t == last_t)
    def _emit_window():
        if tile >= keep:
            wk_ref[0] = k_r[tile - keep:, :]
            wv_ref[0] = v_new[tile - keep:, :]
        else:
            wk_ref[0, 0:keep - tile, :] = ck_ref[0, tile:keep, :]
            wv_ref[0, 0:keep - tile, :] = cv_ref[0, tile:keep, :]
            wk_ref[0, keep - tile:keep, :] = k_r
            wv_ref[0, keep - tile:keep, :] = v_new

    rows = lax.broadcasted_iota(jnp.int32, (A_GROUP * chunk, 1), 0)
    key_col = lax.broadcasted_iota(jnp.int32, (1, win), 1)
    oa_chunks = []
    for c in range(n_ch):
        q_st = jnp.concatenate([qc[c * chunk:(c + 1) * chunk, :] for qc in q_cols], axis=0)
        o_groups = []
        for g, k_ref in enumerate((klo_s, khi_s)):
            s = _dot_nt(q_st, k_ref[c * chunk:c * chunk + win, :])
            if mask_start:
                first_valid = keep - (t * tile + c * chunk)
                s = jnp.where(key_col >= first_valid, s, NEG_INF)
            sk = jnp.full((A_GROUP * chunk, 1), sinks_ref[g * A_GROUP + A_GROUP - 1], f32)
            for j in range(A_GROUP - 2, -1, -1):
                sk = jnp.where(rows < (j + 1) * chunk, sinks_ref[g * A_GROUP + j], sk)
            m = jnp.maximum(jnp.max(s, axis=-1, keepdims=True), sk)
            p = jnp.exp(s - m)
            den = jnp.sum(p, axis=-1, keepdims=True) + jnp.exp(sk - m)
            p = p * (1.0 / den)
            o_groups.append(_dot(p.astype(bf16), v_s[c * chunk:c * chunk + win, :]))
        o_sel = jnp.where(low_half, o_groups[0], o_groups[1])
        oa_chunks.append(jnp.concatenate([o_sel[j * chunk:(j + 1) * chunk, :] for j in range(A_GROUP)], axis=1))
    o_a = jnp.concatenate(oa_chunks, axis=0).astype(bf16) if n_ch > 1 else oa_chunks[0].astype(bf16)
    proj = _dot(o_a, wpa_ref[...])
    gate = _sigmoid(_dot(xb, w_in_ref[:, _C_GT:_C_GT + D_MODEL]))
    merged = gate * proj

    if tile >= keep:
        @pl.when(t != last_t)
        def _carry():
            klo_s[0:keep, :] = klo_s[tile:tile + keep, :]
            khi_s[0:keep, :] = khi_s[tile:tile + keep, :]
            v_s[0:keep, :] = v_s[tile:tile + keep, :]

    mq = _dot(xb, w_in_ref[:, _C_MQ:_C_GT]).astype(bf16)
    om_heads = []
    for h in range(M_HEADS):
        hs = slice(h * M_HEAD_DIM, (h + 1) * M_HEAD_DIM)
        s = _dot_nt(mq[:, hs], mk_ref[0, :, hs]) * (M_HEAD_DIM ** -0.5)
        m = jnp.max(s, axis=-1, keepdims=True)
        p = jnp.exp(s - m)
        p = p * (1.0 / jnp.sum(p, axis=-1, keepdims=True))
        om_heads.append(_dot(p.astype(bf16), mv_ref[0, :, hs]))
    o_m = jnp.concatenate(om_heads, axis=1).astype(bf16)
    proj = _dot(o_m, wpm_ref[...])
    gate = _sigmoid(_dot(xb, w_in_ref[:, _C_GT + 2 * D_MODEL:_C_END]))
    merged = merged + gate * proj

    gk1 = _dot(xb, w_in_ref[:, _C_GK:_C_MQ]).astype(bf16)
    gk = _dot(gk1, w_gk2_ref[...]) + b_gk_ref[...]
    log_a = (jnp.minimum(gk, 0.0) - jnp.log1p(jnp.exp(-jnp.abs(gk)))) * (1.0 / GATE_TAU)
    r_i = lax.broadcasted_iota(jnp.int32, (tile, tile), 0)
    c_i = lax.broadcasted_iota(jnp.int32, (tile, tile), 1)
    if n_ch > 1:
        causal = (c_i <= r_i) & ((r_i // chunk) == (c_i // chunk))
    else:
        causal = c_i <= r_i
    tri = jnp.where(causal, 1.0, 0.0).astype(bf16)
    la_hi = log_a.astype(bf16)
    la_lo = (log_a - la_hi.astype(f32)).astype(bf16)
    cum = _dot(tri, la_hi) + _dot(tri, la_lo)
    last_rows = [cum[(c + 1) * chunk - 1:(c + 1) * chunk, :] for c in range(n_ch)]
    if n_ch > 1:
        last = jnp.concatenate([jnp.broadcast_to(r, (chunk, B_QK)) for r in last_rows], axis=0)
    else:
        last = last_rows[0]
    bq = _dot(xb, w_in_ref[:, _C_BQ:_C_BK])
    bk = _dot(xb, w_in_ref[:, _C_BK:_C_BV])
    q_dec = (bq * (B_DK ** -0.5) * jnp.exp(cum)).astype(bf16)
    k_inv = (bk * jnp.exp(-cum)).astype(bf16)
    k_end = (bk * jnp.exp(last - cum)).astype(bf16)
    decay = [jnp.exp(r) for r in last_rows]
    bv = _dot(xb, w_in_ref[:, _C_BV:_C_BG]).astype(bf16)
    bg = _dot(xb, w_in_ref[:, _C_BG:_C_GK])
    gn = gn_ref[...]
    ob_heads = []
    for h in range(B_HEADS):
        ks = slice(h * B_DK, (h + 1) * B_DK)
        vs = slice(h * B_DV, (h + 1) * B_DV)
        qd, ki, ke, vh = q_dec[:, ks], k_inv[:, ks], k_end[:, ks], bv[:, vs]
        att = jnp.where(causal, _dot_nt(qd, ki), 0.0).astype(bf16)
        o_intra = _dot(att, vh)
        o_inter = []
        state = st_s[h]
        for c in range(n_ch):
            cs = slice(c * chunk, (c + 1) * chunk)
            o_inter.append(_dot_nt(qd[cs, :], state.astype(bf16)))
            state = state * decay[c][:, ks] + _dot_tn(vh[cs, :], ke[cs, :])
        st_s[h] = state
        o = o_intra + (jnp.concatenate(o_inter, axis=0) if n_ch > 1 else o_inter[0])
        o = o * lax.rsqrt(jnp.mean(o * o, axis=-1, keepdims=True) + RMS_EPS) * gn
        g_h = bg[:, vs]
        ob_heads.append((o * (g_h * _sigmoid(g_h))).astype(bf16))
    o_b = jnp.concatenate(ob_heads, axis=1)
    proj = _dot(o_b, wpb_ref[...])
    gate = _sigmoid(_dot(xb, w_in_ref[:, _C_GT + D_MODEL:_C_GT + 2 * D_MODEL]))
    merged = merged + gate * proj

    @pl.when(t == last_t)
    def _emit_state():
        for h in range(B_HEADS):
            st_ref[0, h] = st_s[h].T

    z = alpha * x + _dot(merged.astype(bf16), wout_ref[...])
    y_ref[0] = _layer_norm(z, ln_g_ref[...], ln_b_ref[...])


def _const_spec(shape):
    nd = len(shape)
    return pl.BlockSpec(shape, lambda b, t, _n=nd: (0,) * _n, pipeline_mode=pl.Buffered(1))


def _mixer(x, rope_tabs, mk, mv, cache, lw, *, tile, chunk, alpha):
    nb, seq, _ = x.shape
    n_mem = mk.shape[1]
    keep = WINDOW
    has_cache = cache is not None
    grid = (nb, seq // tile)
    assert seq % tile == 0 and tile % chunk == 0
    assert tile >= keep or (has_cache and grid[1] == 1)

    per_b = lambda b, t: (b, 0, 0)
    in_specs = [
        pl.BlockSpec(memory_space=pltpu.SMEM),
        pl.BlockSpec((1, tile, D_MODEL), lambda b, t: (b, t, 0)),
        pl.BlockSpec((tile, LANES), lambda b, t: (t, 0)),
        pl.BlockSpec((tile, LANES), lambda b, t: (t, 0)),
        pl.BlockSpec((tile, LANES), lambda b, t: (t, 0)),
        pl.BlockSpec((1, n_mem, M_Q), per_b),
        pl.BlockSpec((1, n_mem, M_Q), per_b),
    ]
    args = [lw["sinks"], x, *rope_tabs, mk, mv]
    if has_cache:
        in_specs += [
            pl.BlockSpec((1, keep, LANES), per_b),
            pl.BlockSpec((1, keep, LANES), per_b),
            pl.BlockSpec((1, B_HEADS, B_DK, B_DV), lambda b, t: (b, 0, 0, 0)),
        ]
        args += list(cache)
    weights = [lw["w_in"], lw["w_gk2"], lw["b_gk"], lw["gn"], lw["wpa"], lw["wpb"], lw["wpm"], lw["wout"],
               lw["ln1_g"], lw["ln1_b"]]
    in_specs += [_const_spec(w.shape) for w in weights]
    args += weights

    out_shape = (
        jax.ShapeDtypeStruct((nb, seq, D_MODEL), f32),
        jax.ShapeDtypeStruct((nb, keep, LANES), f32),
        jax.ShapeDtypeStruct((nb, keep, LANES), f32),
        jax.ShapeDtypeStruct((nb, B_HEADS, B_DK, B_DV), f32),
    )
    out_specs = (
        pl.BlockSpec((1, tile, D_MODEL), lambda b, t: (b, t, 0)),
        pl.BlockSpec((1, keep, LANES), per_b),
        pl.BlockSpec((1, keep, LANES), per_b),
        pl.BlockSpec((1, B_HEADS, B_DK, B_DV), lambda b, t: (b, 0, 0, 0)),
    )
    scratch = [
        pltpu.VMEM((keep + tile, LANES), bf16),
        pltpu.VMEM((keep + tile, LANES), bf16),
        pltpu.VMEM((keep + tile, LANES), bf16),
        pltpu.VMEM((B_HEADS, B_DV, B_DK), f32),
    ]
    kern = functools.partial(_mixer_kernel, tile=tile, chunk=chunk, keep=keep, has_cache=has_cache,
                             mask_start=not has_cache, alpha=alpha)
    return pl.pallas_call(
        kern,
        grid=grid,
        in_specs=in_specs,
        out_specs=out_specs,
        out_shape=out_shape,
        scratch_shapes=scratch,
        compiler_params=pltpu.CompilerParams(
            dimension_semantics=("arbitrary", "arbitrary"), vmem_limit_bytes=VMEM_LIMIT_BYTES),
        name="mixer_sample" if has_cache else "mixer_prompt",
    )(*args)


def _mlp_kernel(x_ref, wup_ref, bup_ref, wdn_ref, bdn_ref, g_ref, b_ref, y_ref, *, ff_block, alpha):
    x = x_ref[...]
    xb = x.astype(bf16)
    acc = None
    for j in range(D_FF // ff_block):
        fs = slice(j * ff_block, (j + 1) * ff_block)
        h = jnp.maximum(_dot(xb, wup_ref[:, fs]) + bup_ref[:, fs], 0.0)
        part = _dot((h * h).astype(bf16), wdn_ref[fs, :])
        acc = part if acc is None else acc + part
    z = alpha * x + (acc + bdn_ref[...])
    y_ref[...] = _layer_norm(z, g_ref[...], b_ref[...])


def _mlp(x2d, lw, *, tile, alpha):
    n = x2d.shape[0]
    assert n % tile == 0
    const = lambda shape: pl.BlockSpec(shape, lambda i: (0, 0), pipeline_mode=pl.Buffered(1))
    weights = [lw["wup"], lw["bup"], lw["wdn"], lw["bdn"], lw["ln2_g"], lw["ln2_b"]]
    return pl.pallas_call(
        functools.partial(_mlp_kernel, ff_block=1024, alpha=alpha),
        grid=(n // tile,),
        in_specs=[pl.BlockSpec((tile, D_MODEL), lambda i: (i, 0))] + [const(w.shape) for w in weights],
        out_specs=pl.BlockSpec((tile, D_MODEL), lambda i: (i, 0)),
        out_shape=jax.ShapeDtypeStruct((n, D_MODEL), f32),
        compiler_params=pltpu.CompilerParams(
            dimension_semantics=("arbitrary",), vmem_limit_bytes=VMEM_LIMIT_BYTES),
        name="mlp",
    )(x2d, *weights)


def _memkv_kernel(mem_ref, w_ref, k_ref, v_ref, kb_ref, vb_ref):
    kv = _dot(mem_ref[0].astype(bf16), w_ref[0])
    k_ref[0, 0] = kv[:, :M_Q]
    v_ref[0, 0] = kv[:, M_Q:]
    kb_ref[0, 0] = kv[:, :M_Q].astype(bf16)
    vb_ref[0, 0] = kv[:, M_Q:].astype(bf16)


def _memkv(mem, w_mem_kv_b):
    depth = w_mem_kv_b.shape[0]
    nb, n_mem, _ = mem.shape
    out_spec = pl.BlockSpec((1, 1, n_mem, M_Q), lambda l, b: (l, b, 0, 0))
    return pl.pallas_call(
        _memkv_kernel,
        grid=(depth, nb),
        in_specs=[pl.BlockSpec((1, n_mem, D_MODEL), lambda l, b: (b, 0, 0)),
                  pl.BlockSpec((1, D_MODEL, 2 * M_Q), lambda l, b: (l, 0, 0))],
        out_specs=(out_spec,) * 4,
        out_shape=(jax.ShapeDtypeStruct((depth, nb, n_mem, M_Q), f32),) * 2
        + (jax.ShapeDtypeStruct((depth, nb, n_mem, M_Q), bf16),) * 2,
        compiler_params=pltpu.CompilerParams(
            dimension_semantics=("arbitrary", "arbitrary"), vmem_limit_bytes=VMEM_LIMIT_BYTES),
        name="memkv",
    )(mem, w_mem_kv_b)


def _rope_tables(pos):
    half = ROT_DIM // 2
    inv = ROPE_THETA ** (-jnp.arange(half, dtype=f32) / half)
    ang = pos.astype(f32)[:, None] * inv[None, :]
    cos, sin = jnp.cos(ang), jnp.sin(ang)
    n = pos.shape[0]
    pad = jnp.zeros((n, A_HEAD_DIM - ROT_DIM), f32)
    zero = jnp.zeros((n, half), f32)
    cos_h = jnp.concatenate([cos, cos, jnp.ones_like(pad)], axis=1)
    lo_h = jnp.concatenate([-sin, zero, pad], axis=1)
    hi_h = jnp.concatenate([zero, sin, pad], axis=1)
    return tuple(jnp.concatenate([t_, t_], axis=1) for t_ in (cos_h, lo_h, hi_h))


def _prep_weights(w_in, w_gk2, b_gk, attn_sinks, gla_norm_g, w_proj_a, w_proj_b, w_proj_m, w_out,
                  ln1_g, ln1_b, w_up, b_up, w_down, b_down, ln2_g, ln2_b):
    depth = w_in.shape[0]
    splits = (A_Q, A_KV, A_KV, B_QK, B_QK, B_V, B_V, GATE_RANK, M_Q, N_BRANCH * D_MODEL)
    parts, off = [], 0
    for wdt in splits:
        parts.append(w_in[:, :, off:off + wdt])
        off += wdt
    aq, ak, av, bq, bk, bv, bg, bgk, mq, gates = parts
    head_order = [h for j in range(A_GROUP) for h in (j, j + A_GROUP)]
    aq = aq.reshape(depth, D_MODEL, A_HEADS, A_HEAD_DIM)[:, :, head_order, :].reshape(depth, D_MODEL, A_Q)
    bgk = jnp.pad(bgk, ((0, 0), (0, 0), (0, GK_PAD - GATE_RANK)))
    w_in_r = jnp.concatenate([aq, ak, av, bq, bk, bv, bg, bgk, mq, gates], axis=-1).astype(bf16)
    wpa = w_proj_a.reshape(depth, A_HEADS, A_HEAD_DIM, D_MODEL)[:, head_order].reshape(depth, A_Q, D_MODEL)
    lws = []
    for l in range(depth):
        lws.append(dict(
            sinks=attn_sinks[l],
            w_in=w_in_r[l],
            w_gk2=jnp.pad(w_gk2[l], ((0, GK_PAD - GATE_RANK), (0, 0))).astype(bf16),
            b_gk=b_gk[l][None, :],
            gn=gla_norm_g[l][None, :],
            wpa=wpa[l].astype(bf16),
            wpb=w_proj_b[l].astype(bf16),
            wpm=w_proj_m[l].astype(bf16),
            wout=w_out[l].astype(bf16),
            ln1_g=ln1_g[l][None, :], ln1_b=ln1_b[l][None, :],
            wup=w_up[l].astype(bf16), bup=b_up[l][None, :],
            wdn=w_down[l].astype(bf16), bdn=b_down[l][None, :],
            ln2_g=ln2_g[l][None, :], ln2_b=ln2_b[l][None, :],
        ))
    return lws


def _pick_tile(n, pref):
    tile = min(pref, n)
    while n % tile:
        tile //= 2
    return tile


def kernel(x_prompt, x_sample, cache_win_k, cache_win_v, state_gla, cache_mem_k, cache_mem_v, mem_prompt,
           w_in, w_gk2, b_gk, attn_sinks, gla_norm_g, w_mem_kv, w_proj_a, w_proj_b, w_proj_m, w_out,
           ln1_g, ln1_b, w_up, b_up, w_down, b_down, ln2_g, ln2_b):
    depth = w_in.shape[0]
    alpha = (2 * depth) ** 0.25
    lws = _prep_weights(w_in, w_gk2, b_gk, attn_sinks, gla_norm_g, w_proj_a, w_proj_b, w_proj_m, w_out,
                        ln1_g, ln1_b, w_up, b_up, w_down, b_down, ln2_g, ln2_b)

    bp, sp, _ = x_prompt.shape
    n_mem = mem_prompt.shape[1]
    mk_p, mv_p, mk_b, mv_b = _memkv(mem_prompt, w_mem_kv.astype(bf16))
    rope_p = _rope_tables(jnp.arange(sp, dtype=jnp.int32))
    mix_tile = _pick_tile(sp, 256)
    mlp_tile = _pick_tile(bp * sp, 512)
    x = x_prompt
    wk_p, wv_p, gs_p = [], [], []
    for l in range(depth):
        x, wk, wv, st = _mixer(x, rope_p, mk_b[l], mv_b[l], None, lws[l], tile=mix_tile, chunk=CHUNK, alpha=alpha)
        x = _mlp(x.reshape(bp * sp, D_MODEL), lws[l], tile=mlp_tile, alpha=alpha).reshape(bp, sp, D_MODEL)
        wk_p.append(wk)
        wv_p.append(wv)
        gs_p.append(st)
    y_prompt = x

    bs, ts, _ = x_sample.shape
    keep = cache_win_k.shape[2]
    rope_s = _rope_tables(PAST_LEN + jnp.arange(ts, dtype=jnp.int32))
    x = x_sample
    wk_s, wv_s, gs_s = [], [], []
    for l in range(depth):
        cache = (cache_win_k[l].reshape(bs, keep, A_KV), cache_win_v[l].reshape(bs, keep, A_KV), state_gla[l])
        mk = cache_mem_k[l].reshape(bs, n_mem, M_Q).astype(bf16)
        mv = cache_mem_v[l].reshape(bs, n_mem, M_Q).astype(bf16)
        x, wk, wv, st = _mixer(x, rope_s, mk, mv, cache, lws[l], tile=ts, chunk=ts, alpha=alpha)
        x = _mlp(x.reshape(bs * ts, D_MODEL), lws[l], tile=_pick_tile(bs * ts, 512), alpha=alpha)
        x = x.reshape(bs, ts, D_MODEL)
        wk_s.append(wk)
        wv_s.append(wv)
        gs_s.append(st)
    y_sample = x

    kv_shape = lambda b: (depth, b, keep, A_KV_HEADS, A_HEAD_DIM)
    mem_shape = (depth, bp, n_mem, M_HEADS, M_HEAD_DIM)
    return (y_prompt, y_sample,
            jnp.stack(wk_p).reshape(kv_shape(bp)), jnp.stack(wv_p).reshape(kv_shape(bp)), jnp.stack(gs_p),
            mk_p.reshape(mem_shape), mv_p.reshape(mem_shape),
            jnp.stack(wk_s).reshape(kv_shape(bs)), jnp.stack(wv_s).reshape(kv_shape(bs)), jnp.stack(gs_s))
```

```python
import functools

import jax
import jax.numpy as jnp
from jax import lax
from jax.experimental import pallas as pl
from jax.experimental.pallas import tpu as pltpu

D_MODEL = 1024
PAST_LEN = 1024
CHUNK = 64
WINDOW = 128
A_HEADS = 8
A_KV_HEADS = 2
A_HEAD_DIM = 64
A_GROUP = A_HEADS // A_KV_HEADS
A_SCALE = A_HEAD_DIM ** -0.5
ROT_DIM = A_HEAD_DIM // 4
ROPE_THETA = 500000.0
A_Q = A_HEADS * A_HEAD_DIM
A_KV = A_KV_HEADS * A_HEAD_DIM
B_HEADS = 4
B_DK = 128
B_DV = 256
B_QK = B_HEADS * B_DK
B_V = B_HEADS * B_DV
GATE_RANK = 16
GATE_TAU = 16.0
M_HEADS = 4
M_HEAD_DIM = 128
M_Q = M_HEADS * M_HEAD_DIM
D_FF = 4 * D_MODEL
N_BRANCH = 3
LN_EPS = 1e-5
RMS_EPS = 1e-6
NEG_INF = -1e30

LANES = 128
GK_PAD = LANES
VMEM_LIMIT_BYTES = 60000 * 1024
GATE_BLOCK = 1024
MIXER_ORDER = ("A B M A ga B A M B A M B A M A gm A M A B M A A M B A gb B A M B B B B")

_C_AQ = 0
_C_AK = _C_AQ + A_Q
_C_AV = _C_AK + A_KV
_C_BQ = _C_AV + A_KV
_C_BK = _C_BQ + B_QK
_C_BV = _C_BK + B_QK
_C_BG = _C_BV + B_V
_C_GK = _C_BG + B_V
_C_MQ = _C_GK + GK_PAD
_C_GT = _C_MQ + M_Q
_C_END = _C_GT + N_BRANCH * D_MODEL

_NT = (((1,), (1,)), ((), ()))
_TN = (((0,), (0,)), ((), ()))

bf16 = jnp.bfloat16
f32 = jnp.float32


def _dot(a, b):
    return jnp.dot(a, b, preferred_element_type=f32)


def _dot_nt(a, b):
    return lax.dot_general(a, b, _NT, preferred_element_type=f32)


def _dot_tn(a, b):
    return lax.dot_general(a, b, _TN, preferred_element_type=f32)


def _layer_norm(z, g, b):
    mu = jnp.mean(z, axis=-1, keepdims=True)
    zc = z - mu
    var = jnp.mean(zc * zc, axis=-1, keepdims=True)
    return zc * lax.rsqrt(var + LN_EPS) * g + b


def _sigmoid(z):
    return 1.0 / (1.0 + jnp.exp(-z))


def _mixer_kernel(*refs, tile, chunk, keep, has_cache, mask_start, alpha):
    it = iter(refs)
    sinks_ref = next(it)
    x_ref = next(it)
    cos_ref, sin_lo_ref, sin_hi_ref = next(it), next(it), next(it)
    mk_ref, mv_ref = next(it), next(it)
    if has_cache:
        ck_ref, cv_ref, s0_ref = next(it), next(it), next(it)
    w_in_ref, w_gk2_ref, b_gk_ref, gn_ref = next(it), next(it), next(it), next(it)
    wpa_ref, wpb_ref, wpm_ref, wout_ref = next(it), next(it), next(it), next(it)
    ln_g_ref, ln_b_ref = next(it), next(it)
    y_ref, wk_ref, wv_ref, st_ref = next(it), next(it), next(it), next(it)
    klo_s, khi_s, v_s, st_s = next(it), next(it), next(it), next(it)

    t = pl.program_id(1)
    last_t = pl.num_programs(1) - 1
    n_ch = tile // chunk
    win = keep + chunk

    lane = lax.broadcasted_iota(jnp.int32, (1, LANES), 1)
    low_half = lane < A_HEAD_DIM

    @pl.when(t == 0)
    def _init():
        if has_cache:
            ck = ck_ref[0]
            klo_s[0:keep, :] = jnp.where(low_half, ck, 0.0).astype(bf16)
            khi_s[0:keep, :] = jnp.where(low_half, 0.0, ck).astype(bf16)
            v_s[0:keep, :] = cv_ref[0].astype(bf16)
            for h in range(B_HEADS):
                st_s[h] = s0_ref[0, h].T
        else:
            zero = jnp.zeros((keep, LANES), bf16)
            klo_s[0:keep, :] = zero
            khi_s[0:keep, :] = zero
            v_s[0:keep, :] = zero
            st_s[...] = jnp.zeros(st_s.shape, f32)

    x = x_ref[0]
    xb = x.astype(bf16)
    res = {}

    def gate_task(name, col):
        blocks = []
        for j in range(D_MODEL // GATE_BLOCK):
            cs = slice(col + j * GATE_BLOCK, col + (j + 1) * GATE_BLOCK)
            blocks.append(_sigmoid(_dot(xb, w_in_ref[:, cs])))
            yield
        res[name] = jnp.concatenate(blocks, axis=1)

    def task_a():
        cos = cos_ref[...]
        sin_lo = sin_lo_ref[...]
        sin_hi = sin_hi_ref[...]

        def rope(z):
            return (z * cos + pltpu.roll(z, LANES - ROT_DIM // 2, 1) * sin_lo
                    + pltpu.roll(z, ROT_DIM // 2, 1) * sin_hi)

        qkv = _dot(xb, w_in_ref[:, _C_AQ:_C_BQ])
        yield
        q_cols = [(rope(qkv[:, j * LANES:(j + 1) * LANES]) * A_SCALE).astype(bf16) for j in range(A_GROUP)]
        k_r = rope(qkv[:, _C_AK:_C_AV])
        v_new = qkv[:, _C_AV:_C_BQ]
        klo_s[keep:keep + tile, :] = jnp.where(low_half, k_r, 0.0).astype(bf16)
        khi_s[keep:keep + tile, :] = jnp.where(low_half, 0.0, k_r).astype(bf16)
        v_s[keep:keep + tile, :] = v_new.astype(bf16)

        @pl.when(t == last_t)
        def _emit_window():
            if tile >= keep:
                wk_ref[0] = k_r[tile - keep:, :]
                wv_ref[0] = v_new[tile - keep:, :]
            else:
                wk_ref[0, 0:keep - tile, :] = ck_ref[0, tile:keep, :]
                wv_ref[0, 0:keep - tile, :] = cv_ref[0, tile:keep, :]
                wk_ref[0, keep - tile:keep, :] = k_r
                wv_ref[0, keep - tile:keep, :] = v_new
        yield

        rows = lax.broadcasted_iota(jnp.int32, (A_GROUP * chunk, 1), 0)
        key_col = lax.broadcasted_iota(jnp.int32, (1, win), 1)
        sinks = []
        for g in range(A_KV_HEADS):
            sk = jnp.full((A_GROUP * chunk, 1), sinks_ref[g * A_GROUP + A_GROUP - 1], f32)
            for j in range(A_GROUP - 2, -1, -1):
                sk = jnp.where(rows < (j + 1) * chunk, sinks_ref[g * A_GROUP + j], sk)
            sinks.append(sk)

        def scores(c):
            q_st = jnp.concatenate([qc[c * chunk:(c + 1) * chunk, :] for qc in q_cols], axis=0)
            out = []
            for k_ref in (klo_s, khi_s):
                s = _dot_nt(q_st, k_ref[c * chunk:c * chunk + win, :])
                if mask_start:
                    first_valid = keep - (t * tile + c * chunk)
                    s = jnp.where(key_col >= first_valid, s, NEG_INF)
                out.append(s)
            return out

        def softmax(s, sk):
            m = jnp.maximum(jnp.max(s, axis=-1, keepdims=True), sk)
            p = jnp.exp(s - m)
            den = jnp.sum(p, axis=-1, keepdims=True) + jnp.exp(sk - m)
            return (p * (1.0 / den)).astype(bf16)

        def values(c, probs):
            o_groups = [_dot(p, v_s[c * chunk:c * chunk + win, :]) for p in probs]
            o_sel = jnp.where(low_half, o_groups[0], o_groups[1])
            return jnp.concatenate([o_sel[j * chunk:(j + 1) * chunk, :] for j in range(A_GROUP)], axis=1)

        oa_chunks = []
        s_next = scores(0)
        yield
        for c in range(n_ch):
            s_cur = s_next
            if c + 1 < n_ch:
                s_next = scores(c + 1)
            probs = [softmax(s, sk) for s, sk in zip(s_cur, sinks)]
            yield
            oa_chunks.append(values(c, probs))
            yield
        o_a = jnp.concatenate(oa_chunks, axis=0).astype(bf16) if n_ch > 1 else oa_chunks[0].astype(bf16)

        if tile >= keep:
            @pl.when(t != last_t)
            def _carry():
                klo_s[0:keep, :] = klo_s[tile:tile + keep, :]
                khi_s[0:keep, :] = khi_s[tile:tile + keep, :]
                v_s[0:keep, :] = v_s[tile:tile + keep, :]
        res["proj_a"] = _dot(o_a, wpa_ref[...])

    def task_m():
        mq = _dot(xb, w_in_ref[:, _C_MQ:_C_GT]).astype(bf16)
        yield
        heads = [slice(h * M_HEAD_DIM, (h + 1) * M_HEAD_DIM) for h in range(M_HEADS)]

        def scores(hs):
            return _dot_nt(mq[:, hs], mk_ref[0, :, hs]) * (M_HEAD_DIM ** -0.5)

        def softmax(s):
            m = jnp.max(s, axis=-1, keepdims=True)
            p = jnp.exp(s - m)
            return (p * (1.0 / jnp.sum(p, axis=-1, keepdims=True))).astype(bf16)

        om_heads = []
        for i in range(0, M_HEADS, 2):
            pair = heads[i:i + 2]
            s_pair = [scores(hs) for hs in pair]
            yield
            p_pair = [softmax(s) for s in s_pair]
            yield
            om_heads += [_dot(p, mv_ref[0, :, hs]) for p, hs in zip(p_pair, pair)]
            yield
        o_m = jnp.concatenate(om_heads, axis=1).astype(bf16)
        res["proj_m"] = _dot(o_m, wpm_ref[...])

    def task_b():
        gk1 = _dot(xb, w_in_ref[:, _C_GK:_C_MQ]).astype(bf16)
        gk = _dot(gk1, w_gk2_ref[...]) + b_gk_ref[...]
        log_a = (jnp.minimum(gk, 0.0) - jnp.log1p(jnp.exp(-jnp.abs(gk)))) * (1.0 / GATE_TAU)
        yield
        r_i = lax.broadcasted_iota(jnp.int32, (tile, tile), 0)
        c_i = lax.broadcasted_iota(jnp.int32, (tile, tile), 1)
        if n_ch > 1:
            causal = (c_i <= r_i) & ((r_i // chunk) == (c_i // chunk))
        else:
            causal = c_i <= r_i
        tri = jnp.where(causal, 1.0, 0.0).astype(bf16)
        la_hi = log_a.astype(bf16)
        la_lo = (log_a - la_hi.astype(f32)).astype(bf16)
        cum = _dot(tri, la_hi) + _dot(tri, la_lo)
        last_rows = [cum[(c + 1) * chunk - 1:(c + 1) * chunk, :] for c in range(n_ch)]
        if n_ch > 1:
            last = jnp.concatenate([jnp.broadcast_to(r, (chunk, B_QK)) for r in last_rows], axis=0)
        else:
            last = last_rows[0]
        decay = [jnp.exp(r) for r in last_rows]
        yield
        bq = _dot(xb, w_in_ref[:, _C_BQ:_C_BK])
        q_dec = (bq * (B_DK ** -0.5) * jnp.exp(cum)).astype(bf16)
        yield
        bk = _dot(xb, w_in_ref[:, _C_BK:_C_BV])
        k_inv = (bk * jnp.exp(-cum)).astype(bf16)
        k_end = (bk * jnp.exp(last - cum)).astype(bf16)
        yield
        gn = gn_ref[...]
        ob_heads = []
        for h0 in range(0, B_HEADS, 2):
            pair = (h0, h0 + 1)
            ks = [slice(h * B_DK, (h + 1) * B_DK) for h in pair]
            vv = _dot(xb, w_in_ref[:, _C_BV + h0 * B_DV:_C_BV + (h0 + 2) * B_DV]).astype(bf16)
            vh = [vv[:, :B_DV], vv[:, B_DV:]]
            qd = [q_dec[:, s] for s in ks]
            ke = [k_end[:, s] for s in ks]
            att = [jnp.where(causal, _dot_nt(qd[i], k_inv[:, ks[i]]), 0.0).astype(bf16) for i in range(2)]
            yield
            o_intra = [_dot(att[i], vh[i]) for i in range(2)]
            chunks = [slice(c * chunk, (c + 1) * chunk) for c in range(n_ch)]
            upd = [[_dot_tn(vh[i][cs, :], ke[i][cs, :]) for i in range(2)] for cs in chunks]
            state = [st_s[h] for h in pair]
            o_inter = [[], []]
            for c, cs in enumerate(chunks):
                for i in range(2):
                    o_inter[i].append(_dot_nt(qd[i][cs, :], state[i].astype(bf16)))
                state = [state[i] * decay[c][:, ks[i]] + upd[c][i] for i in range(2)]
            for i, h in enumerate(pair):
                st_s[h] = state[i]
            yield
            gg = _dot(xb, w_in_ref[:, _C_BG + h0 * B_DV:_C_BG + (h0 + 2) * B_DV])
            for i in range(2):
                o = o_intra[i] + (jnp.concatenate(o_inter[i], axis=0) if n_ch > 1 else o_inter[i][0])
                o = o * lax.rsqrt(jnp.mean(o * o, axis=-1, keepdims=True) + RMS_EPS) * gn
                g_h = gg[:, i * B_DV:(i + 1) * B_DV]
                ob_heads.append((o * (g_h * _sigmoid(g_h))).astype(bf16))
            yield
        o_b = jnp.concatenate(ob_heads, axis=1)
        res["proj_b"] = _dot(o_b, wpb_ref[...])

        @pl.when(t == last_t)
        def _emit_state():
            for h in range(B_HEADS):
                st_ref[0, h] = st_s[h].T

    tasks = {
        "A": task_a(), "M": task_m(), "B": task_b(),
        "ga": gate_task("gate_a", _C_GT), "gb": gate_task("gate_b", _C_GT + D_MODEL),
        "gm": gate_task("gate_m", _C_GT + 2 * D_MODEL),
    }
    for name in MIXER_ORDER.split():
        next(tasks[name], None)
    for gen in tasks.values():
        for _ in gen:
            pass

    merged = res["gate_a"] * res["proj_a"] + res["gate_b"] * res["proj_b"] + res["gate_m"] * res["proj_m"]

    z = alpha * x + _dot(merged.astype(bf16), wout_ref[...])
    y_ref[0] = _layer_norm(z, ln_g_ref[...], ln_b_ref[...])


def _const_spec(shape):
    nd = len(shape)
    return pl.BlockSpec(shape, lambda b, t, _n=nd: (0,) * _n, pipeline_mode=pl.Buffered(1))


def _mixer(x, rope_tabs, mk, mv, cache, lw, *, tile, chunk, alpha):
    nb, seq, _ = x.shape
    n_mem = mk.shape[1]
    keep = WINDOW
    has_cache = cache is not None
    grid = (nb, seq // tile)
    assert seq % tile == 0 and tile % chunk == 0
    assert tile >= keep or (has_cache and grid[1] == 1)

    per_b = lambda b, t: (b, 0, 0)
    in_specs = [
        pl.BlockSpec(memory_space=pltpu.SMEM),
        pl.BlockSpec((1, tile, D_MODEL), lambda b, t: (b, t, 0)),
        pl.BlockSpec((tile, LANES), lambda b, t: (t, 0)),
        pl.BlockSpec((tile, LANES), lambda b, t: (t, 0)),
        pl.BlockSpec((tile, LANES), lambda b, t: (t, 0)),
        pl.BlockSpec((1, n_mem, M_Q), per_b),
        pl.BlockSpec((1, n_mem, M_Q), per_b),
    ]
    args = [lw["sinks"], x, *rope_tabs, mk, mv]
    if has_cache:
        in_specs += [
            pl.BlockSpec((1, keep, LANES), per_b),
            pl.BlockSpec((1, keep, LANES), per_b),
            pl.BlockSpec((1, B_HEADS, B_DK, B_DV), lambda b, t: (b, 0, 0, 0)),
        ]
        args += list(cache)
    weights = [lw["w_in"], lw["w_gk2"], lw["b_gk"], lw["gn"], lw["wpa"], lw["wpb"], lw["wpm"], lw["wout"],
               lw["ln1_g"], lw["ln1_b"]]
    in_specs += [_const_spec(w.shape) for w in weights]
    args += weights

    out_shape = (
        jax.ShapeDtypeStruct((nb, seq, D_MODEL), f32),
        jax.ShapeDtypeStruct((nb, keep, LANES), f32),
        jax.ShapeDtypeStruct((nb, keep, LANES), f32),
        jax.ShapeDtypeStruct((nb, B_HEADS, B_DK, B_DV), f32),
    )
    out_specs = (
        pl.BlockSpec((1, tile, D_MODEL), lambda b, t: (b, t, 0)),
        pl.BlockSpec((1, keep, LANES), per_b),
        pl.BlockSpec((1, keep, LANES), per_b),
        pl.BlockSpec((1, B_HEADS, B_DK, B_DV), lambda b, t: (b, 0, 0, 0)),
    )
    scratch = [
        pltpu.VMEM((keep + tile, LANES), bf16),
        pltpu.VMEM((keep + tile, LANES), bf16),
        pltpu.VMEM((keep + tile, LANES), bf16),
        pltpu.VMEM((B_HEADS, B_DV, B_DK), f32),
    ]
    kern = functools.partial(_mixer_kernel, tile=tile, chunk=chunk, keep=keep, has_cache=has_cache,
                             mask_start=not has_cache, alpha=alpha)
    return pl.pallas_call(
        kern,
        grid=grid,
        in_specs=in_specs,
        out_specs=out_specs,
        out_shape=out_shape,
        scratch_shapes=scratch,
        compiler_params=pltpu.CompilerParams(
            dimension_semantics=("arbitrary", "arbitrary"), vmem_limit_bytes=VMEM_LIMIT_BYTES),
        name="mixer_sample" if has_cache else "mixer_prompt",
    )(*args)


def _mlp_kernel(x_ref, wup_ref, bup_ref, wdn_ref, bdn_ref, g_ref, b_ref, y_ref, *, ff_block, alpha):
    x = x_ref[...]
    xb = x.astype(bf16)
    acc = None
    for j in range(D_FF // ff_block):
        fs = slice(j * ff_block, (j + 1) * ff_block)
        h = jnp.maximum(_dot(xb, wup_ref[:, fs]) + bup_ref[:, fs], 0.0)
        part = _dot((h * h).astype(bf16), wdn_ref[fs, :])
        acc = part if acc is None else acc + part
    z = alpha * x + (acc + bdn_ref[...])
    y_ref[...] = _layer_norm(z, g_ref[...], b_ref[...])


def _mlp(x2d, lw, *, tile, alpha):
    n = x2d.shape[0]
    assert n % tile == 0
    const = lambda shape: pl.BlockSpec(shape, lambda i: (0, 0), pipeline_mode=pl.Buffered(1))
    weights = [lw["wup"], lw["bup"], lw["wdn"], lw["bdn"], lw["ln2_g"], lw["ln2_b"]]
    return pl.pallas_call(
        functools.partial(_mlp_kernel, ff_block=1024, alpha=alpha),
        grid=(n // tile,),
        in_specs=[pl.BlockSpec((tile, D_MODEL), lambda i: (i, 0))] + [const(w.shape) for w in weights],
        out_specs=pl.BlockSpec((tile, D_MODEL), lambda i: (i, 0)),
        out_shape=jax.ShapeDtypeStruct((n, D_MODEL), f32),
        compiler_params=pltpu.CompilerParams(
            dimension_semantics=("arbitrary",), vmem_limit_bytes=VMEM_LIMIT_BYTES),
        name="mlp",
    )(x2d, *weights)


def _memkv_kernel(mem_ref, w_ref, k_ref, v_ref, kb_ref, vb_ref):
    kv = _dot(mem_ref[0].astype(bf16), w_ref[0])
    k_ref[0, 0] = kv[:, :M_Q]
    v_ref[0, 0] = kv[:, M_Q:]
    kb_ref[0, 0] = kv[:, :M_Q].astype(bf16)
    vb_ref[0, 0] = kv[:, M_Q:].astype(bf16)


def _memkv(mem, w_mem_kv_b):
    depth = w_mem_kv_b.shape[0]
    nb, n_mem, _ = mem.shape
    out_spec = pl.BlockSpec((1, 1, n_mem, M_Q), lambda l, b: (l, b, 0, 0))
    return pl.pallas_call(
        _memkv_kernel,
        grid=(depth, nb),
        in_specs=[pl.BlockSpec((1, n_mem, D_MODEL), lambda l, b: (b, 0, 0)),
                  pl.BlockSpec((1, D_MODEL, 2 * M_Q), lambda l, b: (l, 0, 0))],
        out_specs=(out_spec,) * 4,
        out_shape=(jax.ShapeDtypeStruct((depth, nb, n_mem, M_Q), f32),) * 2
        + (jax.ShapeDtypeStruct((depth, nb, n_mem, M_Q), bf16),) * 2,
        compiler_params=pltpu.CompilerParams(
            dimension_semantics=("arbitrary", "arbitrary"), vmem_limit_bytes=VMEM_LIMIT_BYTES),
        name="memkv",
    )(mem, w_mem_kv_b)


def _rope_tables(pos):
    half = ROT_DIM // 2
    inv = ROPE_THETA ** (-jnp.arange(half, dtype=f32) / half)
    ang = pos.astype(f32)[:, None] * inv[None, :]
    cos, sin = jnp.cos(ang), jnp.sin(ang)
    n = pos.shape[0]
    pad = jnp.zeros((n, A_HEAD_DIM - ROT_DIM), f32)
    zero = jnp.zeros((n, half), f32)
    cos_h = jnp.concatenate([cos, cos, jnp.ones_like(pad)], axis=1)
    lo_h = jnp.concatenate([-sin, zero, pad], axis=1)
    hi_h = jnp.concatenate([zero, sin, pad], axis=1)
    return tuple(jnp.concatenate([t_, t_], axis=1) for t_ in (cos_h, lo_h, hi_h))


def _prep_weights(w_in, w_gk2, b_gk, attn_sinks, gla_norm_g, w_proj_a, w_proj_b, w_proj_m, w_out,
                  ln1_g, ln1_b, w_up, b_up, w_down, b_down, ln2_g, ln2_b):
    depth = w_in.shape[0]
    splits = (A_Q, A_KV, A_KV, B_QK, B_QK, B_V, B_V, GATE_RANK, M_Q, N_BRANCH * D_MODEL)
    parts, off = [], 0
    for wdt in splits:
        parts.append(w_in[:, :, off:off + wdt])
        off += wdt
    aq, ak, av, bq, bk, bv, bg, bgk, mq, gates = parts
    head_order = [h for j in range(A_GROUP) for h in (j, j + A_GROUP)]
    aq = aq.reshape(depth, D_MODEL, A_HEADS, A_HEAD_DIM)[:, :, head_order, :].reshape(depth, D_MODEL, A_Q)
    bgk = jnp.pad(bgk, ((0, 0), (0, 0), (0, GK_PAD - GATE_RANK)))
    w_in_r = jnp.concatenate([aq, ak, av, bq, bk, bv, bg, bgk, mq, gates], axis=-1).astype(bf16)
    wpa = w_proj_a.reshape(depth, A_HEADS, A_HEAD_DIM, D_MODEL)[:, head_order].reshape(depth, A_Q, D_MODEL)
    lws = []
    for l in range(depth):
        lws.append(dict(
            sinks=attn_sinks[l],
            w_in=w_in_r[l],
            w_gk2=jnp.pad(w_gk2[l], ((0, GK_PAD - GATE_RANK), (0, 0))).astype(bf16),
            b_gk=b_gk[l][None, :],
            gn=gla_norm_g[l][None, :],
            wpa=wpa[l].astype(bf16),
            wpb=w_proj_b[l].astype(bf16),
            wpm=w_proj_m[l].astype(bf16),
            wout=w_out[l].astype(bf16),
            ln1_g=ln1_g[l][None, :], ln1_b=ln1_b[l][None, :],
            wup=w_up[l].astype(bf16), bup=b_up[l][None, :],
            wdn=w_down[l].astype(bf16), bdn=b_down[l][None, :],
            ln2_g=ln2_g[l][None, :], ln2_b=ln2_b[l][None, :],
        ))
    return lws


def _pick_tile(n, pref):
    tile = min(pref, n)
    while n % tile:
        tile //= 2
    return tile


def kernel(x_prompt, x_sample, cache_win_k, cache_win_v, state_gla, cache_mem_k, cache_mem_v, mem_prompt,
           w_in, w_gk2, b_gk, attn_sinks, gla_norm_g, w_mem_kv, w_proj_a, w_proj_b, w_proj_m, w_out,
           ln1_g, ln1_b, w_up, b_up, w_down, b_down, ln2_g, ln2_b):
    depth = w_in.shape[0]
    alpha = (2 * depth) ** 0.25
    lws = _prep_weights(w_in, w_gk2, b_gk, attn_sinks, gla_norm_g, w_proj_a, w_proj_b, w_proj_m, w_out,
                        ln1_g, ln1_b, w_up, b_up, w_down, b_down, ln2_g, ln2_b)

    bp, sp, _ = x_prompt.shape
    n_mem = mem_prompt.shape[1]
    mk_p, mv_p, mk_b, mv_b = _memkv(mem_prompt, w_mem_kv.astype(bf16))
    rope_p = _rope_tables(jnp.arange(sp, dtype=jnp.int32))
    mix_tile = _pick_tile(sp, 256)
    mlp_tile = _pick_tile(bp * sp, 512)
    x = x_prompt
    wk_p, wv_p, gs_p = [], [], []
    for l in range(depth):
        x, wk, wv, st = _mixer(x, rope_p, mk_b[l], mv_b[l], None, lws[l], tile=mix_tile, chunk=CHUNK, alpha=alpha)
        x = _mlp(x.reshape(bp * sp, D_MODEL), lws[l], tile=mlp_tile, alpha=alpha).reshape(bp, sp, D_MODEL)
        wk_p.append(wk)
        wv_p.append(wv)
        gs_p.append(st)
    y_prompt = x

    bs, ts, _ = x_sample.shape
    keep = cache_win_k.shape[2]
    rope_s = _rope_tables(PAST_LEN + jnp.arange(ts, dtype=jnp.int32))
    x = x_sample
    wk_s, wv_s, gs_s = [], [], []
    for l in range(depth):
        cache = (cache_win_k[l].reshape(bs, keep, A_KV), cache_win_v[l].reshape(bs, keep, A_KV), state_gla[l])
        mk = cache_mem_k[l].reshape(bs, n_mem, M_Q).astype(bf16)
        mv = cache_mem_v[l].reshape(bs, n_mem, M_Q).astype(bf16)
        x, wk, wv, st = _mixer(x, rope_s, mk, mv, cache, lws[l], tile=ts, chunk=ts, alpha=alpha)
        x = _mlp(x.reshape(bs * ts, D_MODEL), lws[l], tile=_pick_tile(bs * ts, 512), alpha=alpha)
        x = x.reshape(bs, ts, D_MODEL)
        wk_s.append(wk)
        wv_s.append(wv)
        gs_s.append(st)
    y_sample = x

    kv_shape = lambda b: (depth, b, keep, A_KV_HEADS, A_HEAD_DIM)
    mem_shape = (depth, bp, n_mem, M_HEADS, M_HEAD_DIM)
    return (y_prompt, y_sample,
            jnp.stack(wk_p).reshape(kv_shape(bp)), jnp.stack(wv_p).reshape(kv_shape(bp)), jnp.stack(gs_p),
            mk_p.reshape(mem_shape), mv_p.reshape(mem_shape),
            jnp.stack(wk_s).reshape(kv_shape(bs)), jnp.stack(wv_s).reshape(kv_shape(bs)), jnp.stack(gs_s))
```

```python
import functools

import jax
import jax.numpy as jnp
from jax import lax
from jax.experimental import pallas as pl
from jax.experimental.pallas import tpu as pltpu

D_MODEL = 1024
PAST_LEN = 1024
CHUNK = 64
WINDOW = 128
A_HEADS = 8
A_KV_HEADS = 2
A_HEAD_DIM = 64
A_GROUP = A_HEADS // A_KV_HEADS
A_SCALE = A_HEAD_DIM ** -0.5
ROT_DIM = A_HEAD_DIM // 4
ROPE_THETA = 500000.0
A_Q = A_HEADS * A_HEAD_DIM
A_KV = A_KV_HEADS * A_HEAD_DIM
B_HEADS = 4
B_DK = 128
B_DV = 256
B_QK = B_HEADS * B_DK
B_V = B_HEADS * B_DV
GATE_RANK = 16
GATE_TAU = 16.0
M_HEADS = 4
M_HEAD_DIM = 128
M_Q = M_HEADS * M_HEAD_DIM
D_FF = 4 * D_MODEL
N_BRANCH = 3
LN_EPS = 1e-5
RMS_EPS = 1e-6
NEG_INF = -1e30

LANES = 128
GK_PAD = LANES
VMEM_LIMIT_BYTES = 60000 * 1024
GATE_BLOCK = 1024
MIXER_ORDER = ("A B M A ga B A M B A M B A M A B A M A B M A B A M B A gm B A gb B M B")

_C_AQ = 0
_C_AK = _C_AQ + A_Q
_C_AV = _C_AK + A_KV
_C_GK = _C_AV + A_KV
_C_BQ = _C_GK + GK_PAD
_C_BK = _C_BQ + B_QK
_C_BV = _C_BK + B_QK
_C_BG = _C_BV + B_V
_C_MQ = _C_BG + B_V
_C_GT = _C_MQ + M_Q
_C_END = _C_GT + N_BRANCH * D_MODEL

_NT = (((1,), (1,)), ((), ()))
_TN = (((0,), (0,)), ((), ()))

bf16 = jnp.bfloat16
f32 = jnp.float32


def _dot(a, b):
    return jnp.dot(a, b, preferred_element_type=f32)


def _dot_nt(a, b):
    return lax.dot_general(a, b, _NT, preferred_element_type=f32)


def _dot_tn(a, b):
    return lax.dot_general(a, b, _TN, preferred_element_type=f32)


def _layer_norm(z, g, b):
    mu = jnp.mean(z, axis=-1, keepdims=True)
    zc = z - mu
    var = jnp.mean(zc * zc, axis=-1, keepdims=True)
    return zc * lax.rsqrt(var + LN_EPS) * g + b


def _sigmoid(z):
    return 0.5 * jnp.tanh(0.5 * z) + 0.5


def _mixer_kernel(*refs, tile, chunk, keep, has_cache, mask_start, alpha):
    it = iter(refs)
    sinks_ref = next(it)
    x_ref = next(it)
    cos_ref, sin_lo_ref, sin_hi_ref = next(it), next(it), next(it)
    mk_ref, mv_ref = next(it), next(it)
    if has_cache:
        ck_ref, cv_ref, s0_ref = next(it), next(it), next(it)
    w_in_ref, w_gk2_ref, b_gk_ref, gn_ref = next(it), next(it), next(it), next(it)
    wpa_ref, wpb_ref, wpm_ref, wout_ref = next(it), next(it), next(it), next(it)
    ln_g_ref, ln_b_ref = next(it), next(it)
    y_ref, wk_ref, wv_ref, st_ref = next(it), next(it), next(it), next(it)
    klo_s, khi_s, v_s, st_s = next(it), next(it), next(it), next(it)

    t = pl.program_id(1)
    last_t = pl.num_programs(1) - 1
    n_ch = tile // chunk
    win = keep + chunk

    lane = lax.broadcasted_iota(jnp.int32, (1, LANES), 1)
    low_half = lane < A_HEAD_DIM

    @pl.when(t == 0)
    def _init():
        if has_cache:
            ck = ck_ref[0]
            klo_s[0:keep, :] = jnp.where(low_half, ck, 0.0).astype(bf16)
            khi_s[0:keep, :] = jnp.where(low_half, 0.0, ck).astype(bf16)
            v_s[0:keep, :] = cv_ref[0].astype(bf16)
            for h in range(B_HEADS):
                st_s[h] = s0_ref[0, h].T
        else:
            zero = jnp.zeros((keep, LANES), bf16)
            klo_s[0:keep, :] = zero
            khi_s[0:keep, :] = zero
            v_s[0:keep, :] = zero
            st_s[...] = jnp.zeros(st_s.shape, f32)

    x = x_ref[0]
    xb = x.astype(bf16)
    res = {}

    def gate_task(name, col):
        blocks = []
        for j in range(D_MODEL // GATE_BLOCK):
            cs = slice(col + j * GATE_BLOCK, col + (j + 1) * GATE_BLOCK)
            blocks.append(_sigmoid(_dot(xb, w_in_ref[:, cs])))
            yield
        res[name] = jnp.concatenate(blocks, axis=1)

    def task_a():
        cos = cos_ref[...]
        sin_lo = sin_lo_ref[...]
        sin_hi = sin_hi_ref[...]

        def rope(z):
            return (z * cos + pltpu.roll(z, LANES - ROT_DIM // 2, 1) * sin_lo
                    + pltpu.roll(z, ROT_DIM // 2, 1) * sin_hi)

        qkv = _dot(xb, w_in_ref[:, _C_AQ:_C_BQ])
        res["gk1"] = qkv[:, _C_GK:_C_BQ].astype(bf16)
        yield
        q_cols = [(rope(qkv[:, j * LANES:(j + 1) * LANES]) * A_SCALE).astype(bf16) for j in range(A_GROUP)]
        k_r = rope(qkv[:, _C_AK:_C_AV])
        v_new = qkv[:, _C_AV:_C_GK]
        klo_s[keep:keep + tile, :] = jnp.where(low_half, k_r, 0.0).astype(bf16)
        khi_s[keep:keep + tile, :] = jnp.where(low_half, 0.0, k_r).astype(bf16)
        v_s[keep:keep + tile, :] = v_new.astype(bf16)

        if tile >= keep:
            wk_ref[0] = k_r[tile - keep:, :]
            wv_ref[0] = v_new[tile - keep:, :]
        else:
            wk_ref[0, 0:keep - tile, :] = ck_ref[0, tile:keep, :]
            wv_ref[0, 0:keep - tile, :] = cv_ref[0, tile:keep, :]
            wk_ref[0, keep - tile:keep, :] = k_r
            wv_ref[0, keep - tile:keep, :] = v_new
        yield

        rows = lax.broadcasted_iota(jnp.int32, (A_GROUP * chunk, 1), 0)
        key_col = lax.broadcasted_iota(jnp.int32, (1, win), 1)
        sinks = []
        for g in range(A_KV_HEADS):
            sk = jnp.full((A_GROUP * chunk, 1), sinks_ref[g * A_GROUP + A_GROUP - 1], f32)
            for j in range(A_GROUP - 2, -1, -1):
                sk = jnp.where(rows < (j + 1) * chunk, sinks_ref[g * A_GROUP + j], sk)
            sinks.append(sk)

        def scores(c):
            q_st = jnp.concatenate([qc[c * chunk:(c + 1) * chunk, :] for qc in q_cols], axis=0)
            out = []
            for k_ref in (klo_s, khi_s):
                s = _dot_nt(q_st, k_ref[c * chunk:c * chunk + win, :])
                if mask_start:
                    first_valid = keep - (t * tile + c * chunk)
                    s = jnp.where(key_col >= first_valid, s, NEG_INF)
                out.append(s)
            return out

        def softmax(s, sk):
            m = jnp.maximum(jnp.max(s, axis=-1, keepdims=True), sk)
            p = jnp.exp(s - m)
            den = jnp.sum(p, axis=-1, keepdims=True) + jnp.exp(sk - m)
            return (p * (1.0 / den)).astype(bf16)

        def values(c, probs):
            o_groups = [_dot(p, v_s[c * chunk:c * chunk + win, :]) for p in probs]
            o_sel = jnp.where(low_half, o_groups[0], o_groups[1])
            return jnp.concatenate([o_sel[j * chunk:(j + 1) * chunk, :] for j in range(A_GROUP)], axis=1)

        oa_chunks = []
        s_next = scores(0)
        yield
        for c in range(n_ch):
            s_cur = s_next
            if c + 1 < n_ch:
                s_next = scores(c + 1)
            probs = [softmax(s, sk) for s, sk in zip(s_cur, sinks)]
            yield
            oa_chunks.append(values(c, probs))
            yield
        o_a = jnp.concatenate(oa_chunks, axis=0).astype(bf16) if n_ch > 1 else oa_chunks[0].astype(bf16)

        if tile >= keep:
            klo_s[0:keep, :] = klo_s[tile:tile + keep, :]
            khi_s[0:keep, :] = khi_s[tile:tile + keep, :]
            v_s[0:keep, :] = v_s[tile:tile + keep, :]
        res["proj_a"] = _dot(o_a, wpa_ref[...])

    def task_m():
        mq = _dot(xb, w_in_ref[:, _C_MQ:_C_GT]).astype(bf16)
        yield
        heads = [slice(h * M_HEAD_DIM, (h + 1) * M_HEAD_DIM) for h in range(M_HEADS)]

        def scores(hs):
            return _dot_nt(mq[:, hs], mk_ref[0, :, hs]) * (M_HEAD_DIM ** -0.5)

        def softmax(s):
            m = jnp.max(s, axis=-1, keepdims=True)
            p = jnp.exp(s - m)
            return (p * (1.0 / jnp.sum(p, axis=-1, keepdims=True))).astype(bf16)

        om_heads = []
        for i in range(0, M_HEADS, 2):
            pair = heads[i:i + 2]
            s_pair = [scores(hs) for hs in pair]
            yield
            p_pair = [softmax(s) for s in s_pair]
            yield
            om_heads += [_dot(p, mv_ref[0, :, hs]) for p, hs in zip(p_pair, pair)]
            yield
        o_m = jnp.concatenate(om_heads, axis=1).astype(bf16)
        res["proj_m"] = _dot(o_m, wpm_ref[...])

    def task_b():
        gk = _dot(res["gk1"], w_gk2_ref[...]) + b_gk_ref[...]
        log_a = (jnp.minimum(gk, 0.0) - jnp.log(1.0 + jnp.exp(-jnp.abs(gk)))) * (1.0 / GATE_TAU)
        yield
        r_i = lax.broadcasted_iota(jnp.int32, (tile, tile), 0)
        c_i = lax.broadcasted_iota(jnp.int32, (tile, tile), 1)
        if n_ch > 1:
            causal = (c_i <= r_i) & ((r_i // chunk) == (c_i // chunk))
        else:
            causal = c_i <= r_i
        tri = jnp.where(causal, 1.0, 0.0).astype(bf16)
        la_hi = log_a.astype(bf16)
        la_lo = (log_a - la_hi.astype(f32)).astype(bf16)
        cum = _dot(tri, la_hi) + _dot(tri, la_lo)
        last_rows = [cum[(c + 1) * chunk - 1:(c + 1) * chunk, :] for c in range(n_ch)]
        if n_ch > 1:
            last = jnp.concatenate([jnp.broadcast_to(r, (chunk, B_QK)) for r in last_rows], axis=0)
        else:
            last = last_rows[0]
        decay = [jnp.exp(r) for r in last_rows]
        yield
        bq = _dot(xb, w_in_ref[:, _C_BQ:_C_BK])
        q_dec = (bq * (B_DK ** -0.5) * jnp.exp(cum)).astype(bf16)
        yield
        bk = _dot(xb, w_in_ref[:, _C_BK:_C_BV])
        k_inv = (bk * jnp.exp(-cum)).astype(bf16)
        k_end = (bk * jnp.exp(last - cum)).astype(bf16)
        yield
        gn = gn_ref[...]
        ob_heads = []
        for h0 in range(0, B_HEADS, 2):
            pair = (h0, h0 + 1)
            ks = [slice(h * B_DK, (h + 1) * B_DK) for h in pair]
            vv = _dot(xb, w_in_ref[:, _C_BV + h0 * B_DV:_C_BV + (h0 + 2) * B_DV]).astype(bf16)
            vh = [vv[:, :B_DV], vv[:, B_DV:]]
            qd = [q_dec[:, s] for s in ks]
            ke = [k_end[:, s] for s in ks]
            att = [jnp.where(causal, _dot_nt(qd[i], k_inv[:, ks[i]]), 0.0).astype(bf16) for i in range(2)]
            yield
            o_intra = [_dot(att[i], vh[i]) for i in range(2)]
            chunks = [slice(c * chunk, (c + 1) * chunk) for c in range(n_ch)]
            upd = [[_dot_tn(vh[i][cs, :], ke[i][cs, :]) for i in range(2)] for cs in chunks]
            state = [st_s[h] for h in pair]
            o_inter = [[], []]
            for c, cs in enumerate(chunks):
                for i in range(2):
                    o_inter[i].append(_dot_nt(qd[i][cs, :], state[i].astype(bf16)))
                state = [state[i] * decay[c][:, ks[i]] + upd[c][i] for i in range(2)]
            for i, h in enumerate(pair):
                st_s[h] = state[i]
            yield
            gg = _dot(xb, w_in_ref[:, _C_BG + h0 * B_DV:_C_BG + (h0 + 2) * B_DV])
            for i in range(2):
                o = o_intra[i] + (jnp.concatenate(o_inter[i], axis=0) if n_ch > 1 else o_inter[i][0])
                o = o * lax.rsqrt(jnp.mean(o * o, axis=-1, keepdims=True) + RMS_EPS) * gn
                g_h = gg[:, i * B_DV:(i + 1) * B_DV]
                ob_heads.append((o * (g_h * _sigmoid(g_h))).astype(bf16))
            yield
        o_b = jnp.concatenate(ob_heads, axis=1)
        res["proj_b"] = _dot(o_b, wpb_ref[...])

    tasks = {
        "A": task_a(), "M": task_m(), "B": task_b(),
        "ga": gate_task("gate_a", _C_GT), "gb": gate_task("gate_b", _C_GT + D_MODEL),
        "gm": gate_task("gate_m", _C_GT + 2 * D_MODEL),
    }
    for name in MIXER_ORDER.split():
        next(tasks[name], None)
    for gen in tasks.values():
        for _ in gen:
            pass

    merged = res["gate_a"] * res["proj_a"] + res["gate_b"] * res["proj_b"] + res["gate_m"] * res["proj_m"]

    z = alpha * x + _dot(merged.astype(bf16), wout_ref[...])
    y_ref[0] = _layer_norm(z, ln_g_ref[...], ln_b_ref[...])

    @pl.when(t == last_t)
    def _emit_state():
        for h in range(B_HEADS):
            st_ref[0, h] = st_s[h].T


def _const_spec(shape):
    nd = len(shape)
    return pl.BlockSpec(shape, lambda b, t, _n=nd: (0,) * _n, pipeline_mode=pl.Buffered(1))


def _mixer(x, rope_tabs, mk, mv, cache, lw, *, tile, chunk, alpha):
    nb, seq, _ = x.shape
    n_mem = mk.shape[1]
    keep = WINDOW
    has_cache = cache is not None
    grid = (nb, seq // tile)
    assert seq % tile == 0 and tile % chunk == 0
    assert tile >= keep or (has_cache and grid[1] == 1)

    per_b = lambda b, t: (b, 0, 0)
    in_specs = [
        pl.BlockSpec(memory_space=pltpu.SMEM),
        pl.BlockSpec((1, tile, D_MODEL), lambda b, t: (b, t, 0)),
        pl.BlockSpec((tile, LANES), lambda b, t: (t, 0)),
        pl.BlockSpec((tile, LANES), lambda b, t: (t, 0)),
        pl.BlockSpec((tile, LANES), lambda b, t: (t, 0)),
        pl.BlockSpec((1, n_mem, M_Q), per_b),
        pl.BlockSpec((1, n_mem, M_Q), per_b),
    ]
    args = [lw["sinks"], x, *rope_tabs, mk, mv]
    if has_cache:
        in_specs += [
            pl.BlockSpec((1, keep, LANES), per_b),
            pl.BlockSpec((1, keep, LANES), per_b),
            pl.BlockSpec((1, B_HEADS, B_DK, B_DV), lambda b, t: (b, 0, 0, 0)),
        ]
        args += list(cache)
    weights = [lw["w_in"], lw["w_gk2"], lw["b_gk"], lw["gn"], lw["wpa"], lw["wpb"], lw["wpm"], lw["wout"],
               lw["ln1_g"], lw["ln1_b"]]
    in_specs += [_const_spec(w.shape) for w in weights]
    args += weights

    out_shape = (
        jax.ShapeDtypeStruct((nb, seq, D_MODEL), f32),
        jax.ShapeDtypeStruct((nb, keep, LANES), f32),
        jax.ShapeDtypeStruct((nb, keep, LANES), f32),
        jax.ShapeDtypeStruct((nb, B_HEADS, B_DK, B_DV), f32),
    )
    out_specs = (
        pl.BlockSpec((1, tile, D_MODEL), lambda b, t: (b, t, 0)),
        pl.BlockSpec((1, keep, LANES), per_b),
        pl.BlockSpec((1, keep, LANES), per_b),
        pl.BlockSpec((1, B_HEADS, B_DK, B_DV), lambda b, t: (b, 0, 0, 0)),
    )
    scratch = [
        pltpu.VMEM((keep + tile, LANES), bf16),
        pltpu.VMEM((keep + tile, LANES), bf16),
        pltpu.VMEM((keep + tile, LANES), bf16),
        pltpu.VMEM((B_HEADS, B_DV, B_DK), f32),
    ]
    kern = functools.partial(_mixer_kernel, tile=tile, chunk=chunk, keep=keep, has_cache=has_cache,
                             mask_start=not has_cache, alpha=alpha)
    return pl.pallas_call(
        kern,
        grid=grid,
        in_specs=in_specs,
        out_specs=out_specs,
        out_shape=out_shape,
        scratch_shapes=scratch,
        compiler_params=pltpu.CompilerParams(
            dimension_semantics=("arbitrary", "arbitrary"), vmem_limit_bytes=VMEM_LIMIT_BYTES),
        name="mixer_sample" if has_cache else "mixer_prompt",
    )(*args)


def _mlp_kernel(x_ref, wup_ref, bup_ref, wdn_ref, bdn_ref, g_ref, b_ref, y_ref, *, ff_block, alpha):
    x = x_ref[...]
    xb = x.astype(bf16)
    acc = None
    for j in range(D_FF // ff_block):
        fs = slice(j * ff_block, (j + 1) * ff_block)
        h = jnp.maximum(_dot(xb, wup_ref[:, fs]) + bup_ref[:, fs], 0.0)
        part = _dot((h * h).astype(bf16), wdn_ref[fs, :])
        acc = part if acc is None else acc + part
    z = alpha * x + (acc + bdn_ref[...])
    y_ref[...] = _layer_norm(z, g_ref[...], b_ref[...])


def _mlp(x2d, lw, *, tile, alpha):
    n = x2d.shape[0]
    assert n % tile == 0
    const = lambda shape: pl.BlockSpec(shape, lambda i: (0, 0), pipeline_mode=pl.Buffered(1))
    weights = [lw["wup"], lw["bup"], lw["wdn"], lw["bdn"], lw["ln2_g"], lw["ln2_b"]]
    return pl.pallas_call(
        functools.partial(_mlp_kernel, ff_block=1024, alpha=alpha),
        grid=(n // tile,),
        in_specs=[pl.BlockSpec((tile, D_MODEL), lambda i: (i, 0))] + [const(w.shape) for w in weights],
        out_specs=pl.BlockSpec((tile, D_MODEL), lambda i: (i, 0)),
        out_shape=jax.ShapeDtypeStruct((n, D_MODEL), f32),
        compiler_params=pltpu.CompilerParams(
            dimension_semantics=("arbitrary",), vmem_limit_bytes=VMEM_LIMIT_BYTES),
        name="mlp",
    )(x2d, *weights)


def _memkv_kernel(mem_ref, w_ref, k_ref, v_ref, kb_ref, vb_ref):
    kv = _dot(mem_ref[0].astype(bf16), w_ref[0])
    k_ref[0, 0] = kv[:, :M_Q]
    v_ref[0, 0] = kv[:, M_Q:]
    kb_ref[0, 0] = kv[:, :M_Q].astype(bf16)
    vb_ref[0, 0] = kv[:, M_Q:].astype(bf16)


def _memkv(mem, w_mem_kv_b):
    depth = w_mem_kv_b.shape[0]
    nb, n_mem, _ = mem.shape
    out_spec = pl.BlockSpec((1, 1, n_mem, M_Q), lambda l, b: (l, b, 0, 0))
    return pl.pallas_call(
        _memkv_kernel,
        grid=(depth, nb),
        in_specs=[pl.BlockSpec((1, n_mem, D_MODEL), lambda l, b: (b, 0, 0)),
                  pl.BlockSpec((1, D_MODEL, 2 * M_Q), lambda l, b: (l, 0, 0))],
        out_specs=(out_spec,) * 4,
        out_shape=(jax.ShapeDtypeStruct((depth, nb, n_mem, M_Q), f32),) * 2
        + (jax.ShapeDtypeStruct((depth, nb, n_mem, M_Q), bf16),) * 2,
        compiler_params=pltpu.CompilerParams(
            dimension_semantics=("arbitrary", "arbitrary"), vmem_limit_bytes=VMEM_LIMIT_BYTES),
        name="memkv",
    )(mem, w_mem_kv_b)


def _rope_tables(pos):
    half = ROT_DIM // 2
    inv = ROPE_THETA ** (-jnp.arange(half, dtype=f32) / half)
    ang = pos.astype(f32)[:, None] * inv[None, :]
    cos, sin = jnp.cos(ang), jnp.sin(ang)
    n = pos.shape[0]
    pad = jnp.zeros((n, A_HEAD_DIM - ROT_DIM), f32)
    zero = jnp.zeros((n, half), f32)
    cos_h = jnp.concatenate([cos, cos, jnp.ones_like(pad)], axis=1)
    lo_h = jnp.concatenate([-sin, zero, pad], axis=1)
    hi_h = jnp.concatenate([zero, sin, pad], axis=1)
    return tuple(jnp.concatenate([t_, t_], axis=1) for t_ in (cos_h, lo_h, hi_h))


def _prep_weights(w_in, w_gk2, b_gk, attn_sinks, gla_norm_g, w_proj_a, w_proj_b, w_proj_m, w_out,
                  ln1_g, ln1_b, w_up, b_up, w_down, b_down, ln2_g, ln2_b):
    depth = w_in.shape[0]
    splits = (A_Q, A_KV, A_KV, B_QK, B_QK, B_V, B_V, GATE_RANK, M_Q, N_BRANCH * D_MODEL)
    parts, off = [], 0
    for wdt in splits:
        parts.append(w_in[:, :, off:off + wdt])
        off += wdt
    aq, ak, av, bq, bk, bv, bg, bgk, mq, gates = parts
    head_order = [h for j in range(A_GROUP) for h in (j, j + A_GROUP)]
    aq = aq.reshape(depth, D_MODEL, A_HEADS, A_HEAD_DIM)[:, :, head_order, :].reshape(depth, D_MODEL, A_Q)
    bgk = jnp.pad(bgk, ((0, 0), (0, 0), (0, GK_PAD - GATE_RANK)))
    w_in_r = jnp.concatenate([aq, ak, av, bgk, bq, bk, bv, bg, mq, gates], axis=-1).astype(bf16)
    wpa = w_proj_a.reshape(depth, A_HEADS, A_HEAD_DIM, D_MODEL)[:, head_order].reshape(depth, A_Q, D_MODEL)
    lws = []
    for l in range(depth):
        lws.append(dict(
            sinks=attn_sinks[l],
            w_in=w_in_r[l],
            w_gk2=jnp.pad(w_gk2[l], ((0, GK_PAD - GATE_RANK), (0, 0))).astype(bf16),
            b_gk=b_gk[l][None, :],
            gn=gla_norm_g[l][None, :],
            wpa=wpa[l].astype(bf16),
            wpb=w_proj_b[l].astype(bf16),
            wpm=w_proj_m[l].astype(bf16),
            wout=w_out[l].astype(bf16),
            ln1_g=ln1_g[l][None, :], ln1_b=ln1_b[l][None, :],
            wup=w_up[l].astype(bf16), bup=b_up[l][None, :],
            wdn=w_down[l].astype(bf16), bdn=b_down[l][None, :],
            ln2_g=ln2_g[l][None, :], ln2_b=ln2_b[l][None, :],
        ))
    return lws


def _pick_tile(n, pref):
    tile = min(pref, n)
    while n % tile:
        tile //= 2
    return tile


def kernel(x_prompt, x_sample, cache_win_k, cache_win_v, state_gla, cache_mem_k, cache_mem_v, mem_prompt,
           w_in, w_gk2, b_gk, attn_sinks, gla_norm_g, w_mem_kv, w_proj_a, w_proj_b, w_proj_m, w_out,
           ln1_g, ln1_b, w_up, b_up, w_down, b_down, ln2_g, ln2_b):
    depth = w_in.shape[0]
    alpha = (2 * depth) ** 0.25
    lws = _prep_weights(w_in, w_gk2, b_gk, attn_sinks, gla_norm_g, w_proj_a, w_proj_b, w_proj_m, w_out,
                        ln1_g, ln1_b, w_up, b_up, w_down, b_down, ln2_g, ln2_b)

    bp, sp, _ = x_prompt.shape
    n_mem = mem_prompt.shape[1]
    mk_p, mv_p, mk_b, mv_b = _memkv(mem_prompt, w_mem_kv.astype(bf16))
    rope_p = _rope_tables(jnp.arange(sp, dtype=jnp.int32))
    mix_tile = _pick_tile(sp, 256)
    mlp_tile = _pick_tile(bp * sp, 1024)
    x = x_prompt
    wk_p, wv_p, gs_p = [], [], []
    for l in range(depth):
        x, wk, wv, st = _mixer(x, rope_p, mk_b[l], mv_b[l], None, lws[l], tile=mix_tile, chunk=CHUNK, alpha=alpha)
        x = _mlp(x.reshape(bp * sp, D_MODEL), lws[l], tile=mlp_tile, alpha=alpha).reshape(bp, sp, D_MODEL)
        wk_p.append(wk)
        wv_p.append(wv)
        gs_p.append(st)
    y_prompt = x

    bs, ts, _ = x_sample.shape
    keep = cache_win_k.shape[2]
    rope_s = _rope_tables(PAST_LEN + jnp.arange(ts, dtype=jnp.int32))
    x = x_sample
    wk_s, wv_s, gs_s = [], [], []
    for l in range(depth):
        cache = (cache_win_k[l].reshape(bs, keep, A_KV), cache_win_v[l].reshape(bs, keep, A_KV), state_gla[l])
        mk = cache_mem_k[l].reshape(bs, n_mem, M_Q).astype(bf16)
        mv = cache_mem_v[l].reshape(bs, n_mem, M_Q).astype(bf16)
        x, wk, wv, st = _mixer(x, rope_s, mk, mv, cache, lws[l], tile=ts, chunk=ts, alpha=alpha)
        x = _mlp(x.reshape(bs * ts, D_MODEL), lws[l], tile=_pick_tile(bs * ts, 512), alpha=alpha)
        x = x.reshape(bs, ts, D_MODEL)
        wk_s.append(wk)
        wv_s.append(wv)
        gs_s.append(st)
    y_sample = x

    kv_shape = lambda b: (depth, b, keep, A_KV_HEADS, A_HEAD_DIM)
    mem_shape = (depth, bp, n_mem, M_HEADS, M_HEAD_DIM)
    return (y_prompt, y_sample,
            jnp.stack(wk_p).reshape(kv_shape(bp)), jnp.stack(wv_p).reshape(kv_shape(bp)), jnp.stack(gs_p),
            mk_p.reshape(mem_shape), mv_p.reshape(mem_shape),
            jnp.stack(wk_s).reshape(kv_shape(bs)), jnp.stack(wv_s).reshape(kv_shape(bs)), jnp.stack(gs_s))
```

```python
import functools

import jax
import jax.numpy as jnp
from jax import lax
from jax.experimental import pallas as pl
from jax.experimental.pallas import tpu as pltpu

D_MODEL = 1024
PAST_LEN = 1024
CHUNK = 64
WINDOW = 128
A_HEADS = 8
A_KV_HEADS = 2
A_HEAD_DIM = 64
A_GROUP = A_HEADS // A_KV_HEADS
A_SCALE = A_HEAD_DIM ** -0.5
ROT_DIM = A_HEAD_DIM // 4
ROPE_THETA = 500000.0
A_Q = A_HEADS * A_HEAD_DIM
A_KV = A_KV_HEADS * A_HEAD_DIM
B_HEADS = 4
B_DK = 128
B_DV = 256
B_QK = B_HEADS * B_DK
B_V = B_HEADS * B_DV
GATE_RANK = 16
GATE_TAU = 16.0
M_HEADS = 4
M_HEAD_DIM = 128
M_Q = M_HEADS * M_HEAD_DIM
D_FF = 4 * D_MODEL
N_BRANCH = 3
LN_EPS = 1e-5
RMS_EPS = 1e-6
NEG_INF = -1e30

LANES = 128
GK_PAD = LANES
VMEM_LIMIT_BYTES = 60000 * 1024
GATE_BLOCK = 1024
MIXER_ORDER = ("A B M A ga B A M B A M B A M A B A M A B M A B A M B A gm B A gb B M B")

_C_AQ = 0
_C_AK = _C_AQ + A_Q
_C_AV = _C_AK + A_KV
_C_GK = _C_AV + A_KV
_C_BQ = _C_GK + GK_PAD
_C_BK = _C_BQ + B_QK
_C_BV = _C_BK + B_QK
_C_BG = _C_BV + B_V
_C_MQ = _C_BG + B_V
_C_GT = _C_MQ + M_Q
_C_END = _C_GT + N_BRANCH * D_MODEL

_NT = (((1,), (1,)), ((), ()))
_TN = (((0,), (0,)), ((), ()))

bf16 = jnp.bfloat16
f32 = jnp.float32


def _dot(a, b):
    return jnp.dot(a, b, preferred_element_type=f32)


def _dot_nt(a, b):
    return lax.dot_general(a, b, _NT, preferred_element_type=f32)


def _dot_tn(a, b):
    return lax.dot_general(a, b, _TN, preferred_element_type=f32)


def _layer_norm(z, g, b):
    mu = jnp.mean(z, axis=-1, keepdims=True)
    zc = z - mu
    var = jnp.mean(zc * zc, axis=-1, keepdims=True)
    return zc * lax.rsqrt(var + LN_EPS) * g + b


def _sigmoid(z):
    return 0.5 * jnp.tanh(0.5 * z) + 0.5


def _mixer_kernel(*refs, layer, tile, chunk, keep, has_cache, mask_start, alpha):
    it = iter(refs)
    sinks_ref = next(it)
    x_ref = next(it)
    cos_ref, sin_lo_ref, sin_hi_ref = next(it), next(it), next(it)
    mk_ref, mv_ref = next(it), next(it)
    if has_cache:
        ck_ref, cv_ref, s0_ref = next(it), next(it), next(it)
    w_in_ref, w_gk2_ref, b_gk_ref, gn_ref = next(it), next(it), next(it), next(it)
    wpa_ref, wpb_ref, wpm_ref, wout_ref = next(it), next(it), next(it), next(it)
    ln_g_ref, ln_b_ref = next(it), next(it)
    y_ref, wk_ref, wv_ref, st_ref = next(it), next(it), next(it), next(it)
    klo_s, khi_s, v_s, st_s = next(it), next(it), next(it), next(it)

    t = pl.program_id(1)
    last_t = pl.num_programs(1) - 1
    n_ch = tile // chunk
    win = keep + chunk

    lane = lax.broadcasted_iota(jnp.int32, (1, LANES), 1)
    low_half = lane < A_HEAD_DIM

    @pl.when(t == 0)
    def _init():
        if has_cache:
            ck = ck_ref[0]
            klo_s[0:keep, :] = jnp.where(low_half, ck, 0.0).astype(bf16)
            khi_s[0:keep, :] = jnp.where(low_half, 0.0, ck).astype(bf16)
            v_s[0:keep, :] = cv_ref[0].astype(bf16)
            for h in range(B_HEADS):
                st_s[h] = s0_ref[0, h].T
        else:
            zero = jnp.zeros((keep, LANES), bf16)
            klo_s[0:keep, :] = zero
            khi_s[0:keep, :] = zero
            v_s[0:keep, :] = zero
            st_s[...] = jnp.zeros(st_s.shape, f32)

    x = x_ref[0]
    xb = x.astype(bf16)
    res = {}

    def gate_task(name, col):
        blocks = []
        for j in range(D_MODEL // GATE_BLOCK):
            cs = slice(col + j * GATE_BLOCK, col + (j + 1) * GATE_BLOCK)
            blocks.append(_sigmoid(_dot(xb, w_in_ref[:, cs])))
            yield
        res[name] = jnp.concatenate(blocks, axis=1)

    def task_a():
        cos = cos_ref[...]
        sin_lo = sin_lo_ref[...]
        sin_hi = sin_hi_ref[...]

        def rope(z):
            return (z * cos + pltpu.roll(z, LANES - ROT_DIM // 2, 1) * sin_lo
                    + pltpu.roll(z, ROT_DIM // 2, 1) * sin_hi)

        qkv = _dot(xb, w_in_ref[:, _C_AQ:_C_BQ])
        res["gk1"] = qkv[:, _C_GK:_C_BQ].astype(bf16)
        yield
        q_cols = [(rope(qkv[:, j * LANES:(j + 1) * LANES]) * A_SCALE).astype(bf16) for j in range(A_GROUP)]
        k_r = rope(qkv[:, _C_AK:_C_AV])
        v_new = qkv[:, _C_AV:_C_GK]
        klo_s[keep:keep + tile, :] = jnp.where(low_half, k_r, 0.0).astype(bf16)
        khi_s[keep:keep + tile, :] = jnp.where(low_half, 0.0, k_r).astype(bf16)
        v_s[keep:keep + tile, :] = v_new.astype(bf16)

        if tile >= keep:
            wk_ref[0] = k_r[tile - keep:, :]
            wv_ref[0] = v_new[tile - keep:, :]
        else:
            wk_ref[0, 0:keep - tile, :] = ck_ref[0, tile:keep, :]
            wv_ref[0, 0:keep - tile, :] = cv_ref[0, tile:keep, :]
            wk_ref[0, keep - tile:keep, :] = k_r
            wv_ref[0, keep - tile:keep, :] = v_new
        yield

        rows = lax.broadcasted_iota(jnp.int32, (A_GROUP * chunk, 1), 0)
        key_col = lax.broadcasted_iota(jnp.int32, (1, win), 1)
        sinks = []
        for g in range(A_KV_HEADS):
            sk = jnp.full((A_GROUP * chunk, 1), sinks_ref[layer, g * A_GROUP + A_GROUP - 1], f32)
            for j in range(A_GROUP - 2, -1, -1):
                sk = jnp.where(rows < (j + 1) * chunk, sinks_ref[layer, g * A_GROUP + j], sk)
            sinks.append(sk)

        def scores(c):
            q_st = jnp.concatenate([qc[c * chunk:(c + 1) * chunk, :] for qc in q_cols], axis=0)
            out = []
            for k_ref in (klo_s, khi_s):
                s = _dot_nt(q_st, k_ref[c * chunk:c * chunk + win, :])
                if mask_start:
                    first_valid = keep - (t * tile + c * chunk)
                    s = jnp.where(key_col >= first_valid, s, NEG_INF)
                out.append(s)
            return out

        def softmax(s, sk):
            m = jnp.maximum(jnp.max(s, axis=-1, keepdims=True), sk)
            p = jnp.exp(s - m)
            den = jnp.sum(p, axis=-1, keepdims=True) + jnp.exp(sk - m)
            return (p * (1.0 / den)).astype(bf16)

        def values(c, probs):
            o_groups = [_dot(p, v_s[c * chunk:c * chunk + win, :]) for p in probs]
            o_sel = jnp.where(low_half, o_groups[0], o_groups[1])
            return jnp.concatenate([o_sel[j * chunk:(j + 1) * chunk, :] for j in range(A_GROUP)], axis=1)

        oa_chunks = []
        s_next = scores(0)
        yield
        for c in range(n_ch):
            s_cur = s_next
            if c + 1 < n_ch:
                s_next = scores(c + 1)
            probs = [softmax(s, sk) for s, sk in zip(s_cur, sinks)]
            yield
            oa_chunks.append(values(c, probs))
            yield
        o_a = jnp.concatenate(oa_chunks, axis=0).astype(bf16) if n_ch > 1 else oa_chunks[0].astype(bf16)

        if tile >= keep:
            klo_s[0:keep, :] = klo_s[tile:tile + keep, :]
            khi_s[0:keep, :] = khi_s[tile:tile + keep, :]
            v_s[0:keep, :] = v_s[tile:tile + keep, :]
        res["proj_a"] = _dot(o_a, wpa_ref[...])

    def task_m():
        mq = _dot(xb, w_in_ref[:, _C_MQ:_C_GT]).astype(bf16)
        yield
        heads = [slice(h * M_HEAD_DIM, (h + 1) * M_HEAD_DIM) for h in range(M_HEADS)]

        def scores(hs):
            return _dot_nt(mq[:, hs], mk_ref[0, :, hs]) * (M_HEAD_DIM ** -0.5)

        def softmax(s):
            m = jnp.max(s, axis=-1, keepdims=True)
            p = jnp.exp(s - m)
            return (p * (1.0 / jnp.sum(p, axis=-1, keepdims=True))).astype(bf16)

        om_heads = []
        for i in range(0, M_HEADS, 2):
            pair = heads[i:i + 2]
            s_pair = [scores(hs) for hs in pair]
            yield
            p_pair = [softmax(s) for s in s_pair]
            yield
            om_heads += [_dot(p, mv_ref[0, :, hs]) for p, hs in zip(p_pair, pair)]
            yield
        o_m = jnp.concatenate(om_heads, axis=1).astype(bf16)
        res["proj_m"] = _dot(o_m, wpm_ref[...])

    def task_b():
        gk = _dot(res["gk1"], w_gk2_ref[...]) + b_gk_ref[...]
        log_a = (jnp.minimum(gk, 0.0) - jnp.log(1.0 + jnp.exp(-jnp.abs(gk)))) * (1.0 / GATE_TAU)
        yield
        r_i = lax.broadcasted_iota(jnp.int32, (tile, tile), 0)
        c_i = lax.broadcasted_iota(jnp.int32, (tile, tile), 1)
        if n_ch > 1:
            causal = (c_i <= r_i) & ((r_i // chunk) == (c_i // chunk))
        else:
            causal = c_i <= r_i
        tri = jnp.where(causal, 1.0, 0.0).astype(bf16)
        la_hi = log_a.astype(bf16)
        la_lo = (log_a - la_hi.astype(f32)).astype(bf16)
        cum = _dot(tri, la_hi) + _dot(tri, la_lo)
        last_rows = [cum[(c + 1) * chunk - 1:(c + 1) * chunk, :] for c in range(n_ch)]
        if n_ch > 1:
            last = jnp.concatenate([jnp.broadcast_to(r, (chunk, B_QK)) for r in last_rows], axis=0)
        else:
            last = last_rows[0]
        decay = [jnp.exp(r) for r in last_rows]
        yield
        bq = _dot(xb, w_in_ref[:, _C_BQ:_C_BK])
        q_dec = (bq * (B_DK ** -0.5) * jnp.exp(cum)).astype(bf16)
        yield
        bk = _dot(xb, w_in_ref[:, _C_BK:_C_BV])
        k_inv = (bk * jnp.exp(-cum)).astype(bf16)
        k_end = (bk * jnp.exp(last - cum)).astype(bf16)
        yield
        gn = gn_ref[...]
        ob_heads = []
        for h0 in range(0, B_HEADS, 2):
            pair = (h0, h0 + 1)
            ks = [slice(h * B_DK, (h + 1) * B_DK) for h in pair]
            vv = _dot(xb, w_in_ref[:, _C_BV + h0 * B_DV:_C_BV + (h0 + 2) * B_DV]).astype(bf16)
            vh = [vv[:, :B_DV], vv[:, B_DV:]]
            qd = [q_dec[:, s] for s in ks]
            ke = [k_end[:, s] for s in ks]
            att = [jnp.where(causal, _dot_nt(qd[i], k_inv[:, ks[i]]), 0.0).astype(bf16) for i in range(2)]
            yield
            o_intra = [_dot(att[i], vh[i]) for i in range(2)]
            chunks = [slice(c * chunk, (c + 1) * chunk) for c in range(n_ch)]
            upd = [[_dot_tn(vh[i][cs, :], ke[i][cs, :]) for i in range(2)] for cs in chunks]
            state = [st_s[h] for h in pair]
            o_inter = [[], []]
            for c, cs in enumerate(chunks):
                for i in range(2):
                    o_inter[i].append(_dot_nt(qd[i][cs, :], state[i].astype(bf16)))
                state = [state[i] * decay[c][:, ks[i]] + upd[c][i] for i in range(2)]
            for i, h in enumerate(pair):
                st_s[h] = state[i]
            yield
            gg = _dot(xb, w_in_ref[:, _C_BG + h0 * B_DV:_C_BG + (h0 + 2) * B_DV])
            for i in range(2):
                o = o_intra[i] + (jnp.concatenate(o_inter[i], axis=0) if n_ch > 1 else o_inter[i][0])
                o = o * lax.rsqrt(jnp.mean(o * o, axis=-1, keepdims=True) + RMS_EPS) * gn
                g_h = gg[:, i * B_DV:(i + 1) * B_DV]
                ob_heads.append((o * (g_h * _sigmoid(g_h))).astype(bf16))
            yield
        o_b = jnp.concatenate(ob_heads, axis=1)
        res["proj_b"] = _dot(o_b, wpb_ref[...])

    tasks = {
        "A": task_a(), "M": task_m(), "B": task_b(),
        "ga": gate_task("gate_a", _C_GT), "gb": gate_task("gate_b", _C_GT + D_MODEL),
        "gm": gate_task("gate_m", _C_GT + 2 * D_MODEL),
    }
    for name in MIXER_ORDER.split():
        next(tasks[name], None)
    for gen in tasks.values():
        for _ in gen:
            pass

    merged = res["gate_a"] * res["proj_a"] + res["gate_b"] * res["proj_b"] + res["gate_m"] * res["proj_m"]

    z = alpha * x + _dot(merged.astype(bf16), wout_ref[...])
    y_ref[0] = _layer_norm(z, ln_g_ref[...], ln_b_ref[...])

    @pl.when(t == last_t)
    def _emit_state():
        for h in range(B_HEADS):
            st_ref[0, h] = st_s[h].T


def _layer_spec(stacked, layer):
    rest = stacked.shape[1:]
    index = lambda *_: (layer,) + (0,) * len(rest)
    return pl.BlockSpec((None,) + rest, index, pipeline_mode=pl.Buffered(1))


def _mixer(x, rope_tabs, mk, mv, cache, lw, layer, *, tile, chunk, alpha):
    nb, seq, _ = x.shape
    n_mem = mk.shape[2]
    keep = WINDOW
    has_cache = cache is not None
    grid = (nb, seq // tile)
    assert seq % tile == 0 and tile % chunk == 0
    assert tile >= keep or (has_cache and grid[1] == 1)

    per_b = lambda b, t: (b, 0, 0)
    layer_b = lambda b, t: (layer, b, 0, 0)
    in_specs = [
        pl.BlockSpec(memory_space=pltpu.SMEM),
        pl.BlockSpec((1, tile, D_MODEL), lambda b, t: (b, t, 0)),
        pl.BlockSpec((tile, LANES), lambda b, t: (t, 0)),
        pl.BlockSpec((tile, LANES), lambda b, t: (t, 0)),
        pl.BlockSpec((tile, LANES), lambda b, t: (t, 0)),
        pl.BlockSpec((None, 1, n_mem, M_Q), layer_b),
        pl.BlockSpec((None, 1, n_mem, M_Q), layer_b),
    ]
    args = [lw["sinks"], x, *rope_tabs, mk, mv]
    if has_cache:
        in_specs += [
            pl.BlockSpec((None, 1, keep, LANES), layer_b),
            pl.BlockSpec((None, 1, keep, LANES), layer_b),
            pl.BlockSpec((None, 1, B_HEADS, B_DK, B_DV), lambda b, t: (layer, b, 0, 0, 0)),
        ]
        args += list(cache)
    weights = [lw["w_in"], lw["w_gk2"], lw["b_gk"], lw["gn"], lw["wpa"], lw["wpb"], lw["wpm"], lw["wout"],
               lw["ln1_g"], lw["ln1_b"]]
    in_specs += [_layer_spec(w, layer) for w in weights]
    args += weights

    out_shape = (
        jax.ShapeDtypeStruct((nb, seq, D_MODEL), f32),
        jax.ShapeDtypeStruct((nb, keep, LANES), f32),
        jax.ShapeDtypeStruct((nb, keep, LANES), f32),
        jax.ShapeDtypeStruct((nb, B_HEADS, B_DK, B_DV), f32),
    )
    out_specs = (
        pl.BlockSpec((1, tile, D_MODEL), lambda b, t: (b, t, 0)),
        pl.BlockSpec((1, keep, LANES), per_b),
        pl.BlockSpec((1, keep, LANES), per_b),
        pl.BlockSpec((1, B_HEADS, B_DK, B_DV), lambda b, t: (b, 0, 0, 0)),
    )
    scratch = [
        pltpu.VMEM((keep + tile, LANES), bf16),
        pltpu.VMEM((keep + tile, LANES), bf16),
        pltpu.VMEM((keep + tile, LANES), bf16),
        pltpu.VMEM((B_HEADS, B_DV, B_DK), f32),
    ]
    kern = functools.partial(_mixer_kernel, layer=layer, tile=tile, chunk=chunk, keep=keep, has_cache=has_cache,
                             mask_start=not has_cache, alpha=alpha)
    return pl.pallas_call(
        kern,
        grid=grid,
        in_specs=in_specs,
        out_specs=out_specs,
        out_shape=out_shape,
        scratch_shapes=scratch,
        compiler_params=pltpu.CompilerParams(
            dimension_semantics=("arbitrary", "arbitrary"), vmem_limit_bytes=VMEM_LIMIT_BYTES),
        name="mixer_sample" if has_cache else "mixer_prompt",
    )(*args)


def _mlp_kernel(x_ref, wup_ref, bup_ref, wdn_ref, bdn_ref, g_ref, b_ref, y_ref, *, ff_block, alpha):
    x = x_ref[...]
    xb = x.astype(bf16)
    acc = None
    for j in range(D_FF // ff_block):
        fs = slice(j * ff_block, (j + 1) * ff_block)
        h = jnp.maximum(_dot(xb, wup_ref[:, fs]) + bup_ref[:, fs], 0.0)
        part = _dot((h * h).astype(bf16), wdn_ref[fs, :])
        acc = part if acc is None else acc + part
    z = alpha * x + (acc + bdn_ref[...])
    y_ref[...] = _layer_norm(z, g_ref[...], b_ref[...])


def _mlp(x2d, lw, layer, *, tile, alpha):
    n = x2d.shape[0]
    assert n % tile == 0
    weights = [lw["wup"], lw["bup"], lw["wdn"], lw["bdn"], lw["ln2_g"], lw["ln2_b"]]
    return pl.pallas_call(
        functools.partial(_mlp_kernel, ff_block=1024, alpha=alpha),
        grid=(n // tile,),
        in_specs=[pl.BlockSpec((tile, D_MODEL), lambda i: (i, 0))] + [_layer_spec(w, layer) for w in weights],
        out_specs=pl.BlockSpec((tile, D_MODEL), lambda i: (i, 0)),
        out_shape=jax.ShapeDtypeStruct((n, D_MODEL), f32),
        compiler_params=pltpu.CompilerParams(
            dimension_semantics=("arbitrary",), vmem_limit_bytes=VMEM_LIMIT_BYTES),
        name="mlp",
    )(x2d, *weights)


def _memkv_kernel(mem_ref, w_ref, k_ref, v_ref, kb_ref, vb_ref):
    rows, n_mem, _ = mem_ref.shape
    kv = _dot(mem_ref[...].reshape(rows * n_mem, D_MODEL).astype(bf16), w_ref[...])
    k = kv[:, :M_Q].reshape(rows, n_mem, M_Q)
    v = kv[:, M_Q:].reshape(rows, n_mem, M_Q)
    k_ref[...] = k
    v_ref[...] = v
    kb_ref[...] = k.astype(bf16)
    vb_ref[...] = v.astype(bf16)


def _memkv(mem, w_mem_kv_b):
    depth = w_mem_kv_b.shape[0]
    nb, n_mem, _ = mem.shape
    rows = _pick_tile(nb, 4)
    out_spec = pl.BlockSpec((None, rows, n_mem, M_Q), lambda l, b: (l, b, 0, 0))
    return pl.pallas_call(
        _memkv_kernel,
        grid=(depth, nb // rows),
        in_specs=[pl.BlockSpec((rows, n_mem, D_MODEL), lambda l, b: (b, 0, 0)),
                  pl.BlockSpec((None, D_MODEL, 2 * M_Q), lambda l, b: (l, 0, 0))],
        out_specs=(out_spec,) * 4,
        out_shape=(jax.ShapeDtypeStruct((depth, nb, n_mem, M_Q), f32),) * 2
        + (jax.ShapeDtypeStruct((depth, nb, n_mem, M_Q), bf16),) * 2,
        compiler_params=pltpu.CompilerParams(
            dimension_semantics=("arbitrary", "arbitrary"), vmem_limit_bytes=VMEM_LIMIT_BYTES),
        name="memkv",
    )(mem, w_mem_kv_b)


def _rope_tables(pos):
    half = ROT_DIM // 2
    inv = ROPE_THETA ** (-jnp.arange(half, dtype=f32) / half)
    ang = pos.astype(f32)[:, None] * inv[None, :]
    cos, sin = jnp.cos(ang), jnp.sin(ang)
    n = pos.shape[0]
    pad = jnp.zeros((n, A_HEAD_DIM - ROT_DIM), f32)
    zero = jnp.zeros((n, half), f32)
    cos_h = jnp.concatenate([cos, cos, jnp.ones_like(pad)], axis=1)
    lo_h = jnp.concatenate([-sin, zero, pad], axis=1)
    hi_h = jnp.concatenate([zero, sin, pad], axis=1)
    return tuple(jnp.concatenate([t_, t_], axis=1) for t_ in (cos_h, lo_h, hi_h))


def _prep_weights(w_in, w_gk2, b_gk, attn_sinks, gla_norm_g, w_proj_a, w_proj_b, w_proj_m, w_out,
                  ln1_g, ln1_b, w_up, b_up, w_down, b_down, ln2_g, ln2_b):
    wb = w_in.astype(bf16)
    col = lambda lo, hi: wb[:, :, lo:hi]
    src_bq = A_Q + 2 * A_KV
    src_gk = src_bq + 2 * B_QK + 2 * B_V
    src_mq = src_gk + GATE_RANK
    head_order = [h for j in range(A_GROUP) for h in (j, j + A_GROUP)]
    aq = [col(h * A_HEAD_DIM, (h + 1) * A_HEAD_DIM) for h in head_order]
    gk = jnp.pad(col(src_gk, src_mq), ((0, 0), (0, 0), (0, GK_PAD - GATE_RANK)))
    w_in_r = jnp.concatenate(aq + [col(A_Q, src_bq), gk, col(src_bq, src_gk), col(src_mq, wb.shape[2])], axis=-1)
    assert w_in_r.shape[2] == _C_END
    wpa = jnp.concatenate([w_proj_a[:, h * A_HEAD_DIM:(h + 1) * A_HEAD_DIM, :] for h in head_order], axis=1)
    row = lambda v: v[:, None, :]
    return dict(
        sinks=attn_sinks,
        w_in=w_in_r,
        w_gk2=jnp.pad(w_gk2, ((0, 0), (0, GK_PAD - GATE_RANK), (0, 0))).astype(bf16),
        b_gk=row(b_gk), gn=row(gla_norm_g),
        wpa=wpa.astype(bf16), wpb=w_proj_b.astype(bf16), wpm=w_proj_m.astype(bf16), wout=w_out.astype(bf16),
        ln1_g=row(ln1_g), ln1_b=row(ln1_b),
        wup=w_up.astype(bf16), bup=row(b_up), wdn=w_down.astype(bf16), bdn=row(b_down),
        ln2_g=row(ln2_g), ln2_b=row(ln2_b),
    )


def _pick_tile(n, pref):
    tile = min(pref, n)
    while n % tile:
        tile //= 2
    return tile


def kernel(x_prompt, x_sample, cache_win_k, cache_win_v, state_gla, cache_mem_k, cache_mem_v, mem_prompt,
           w_in, w_gk2, b_gk, attn_sinks, gla_norm_g, w_mem_kv, w_proj_a, w_proj_b, w_proj_m, w_out,
           ln1_g, ln1_b, w_up, b_up, w_down, b_down, ln2_g, ln2_b):
    depth = w_in.shape[0]
    alpha = (2 * depth) ** 0.25
    lws = _prep_weights(w_in, w_gk2, b_gk, attn_sinks, gla_norm_g, w_proj_a, w_proj_b, w_proj_m, w_out,
                        ln1_g, ln1_b, w_up, b_up, w_down, b_down, ln2_g, ln2_b)

    bp, sp, _ = x_prompt.shape
    n_mem = mem_prompt.shape[1]
    mk_p, mv_p, mk_b, mv_b = _memkv(mem_prompt, w_mem_kv.astype(bf16))
    rope_p = _rope_tables(jnp.arange(sp, dtype=jnp.int32))
    mix_tile = _pick_tile(sp, 256)
    mlp_tile = _pick_tile(bp * sp, 1024)
    x = x_prompt
    wk_p, wv_p, gs_p = [], [], []
    for l in range(depth):
        x, wk, wv, st = _mixer(x, rope_p, mk_b, mv_b, None, lws, l, tile=mix_tile, chunk=CHUNK, alpha=alpha)
        x = _mlp(x.reshape(bp * sp, D_MODEL), lws, l, tile=mlp_tile, alpha=alpha).reshape(bp, sp, D_MODEL)
        wk_p.append(wk)
        wv_p.append(wv)
        gs_p.append(st)
    y_prompt = x

    bs, ts, _ = x_sample.shape
    keep = cache_win_k.shape[2]
    rope_s = _rope_tables(PAST_LEN + jnp.arange(ts, dtype=jnp.int32))
    cache = (cache_win_k.reshape(depth, bs, keep, A_KV), cache_win_v.reshape(depth, bs, keep, A_KV), state_gla)
    mk_s = cache_mem_k.reshape(depth, bs, n_mem, M_Q).astype(bf16)
    mv_s = cache_mem_v.reshape(depth, bs, n_mem, M_Q).astype(bf16)
    x = x_sample
    wk_s, wv_s, gs_s = [], [], []
    for l in range(depth):
        x, wk, wv, st = _mixer(x, rope_s, mk_s, mv_s, cache, lws, l, tile=ts, chunk=ts, alpha=alpha)
        x = _mlp(x.reshape(bs * ts, D_MODEL), lws, l, tile=_pick_tile(bs * ts, 512), alpha=alpha)
        x = x.reshape(bs, ts, D_MODEL)
        wk_s.append(wk)
        wv_s.append(wv)
        gs_s.append(st)
    y_sample = x

    kv_shape = lambda b: (depth, b, keep, A_KV_HEADS, A_HEAD_DIM)
    mem_shape = (depth, bp, n_mem, M_HEADS, M_HEAD_DIM)
    return (y_prompt, y_sample,
            jnp.stack(wk_p).reshape(kv_shape(bp)), jnp.stack(wv_p).reshape(kv_shape(bp)), jnp.stack(gs_p),
            mk_p.reshape(mem_shape), mv_p.reshape(mem_shape),
            jnp.stack(wk_s).reshape(kv_shape(bs)), jnp.stack(wv_s).reshape(kv_shape(bs)), jnp.stack(gs_s))
```

```python
import functools

import jax
import jax.numpy as jnp
from jax import lax
from jax.experimental import pallas as pl
from jax.experimental.pallas import tpu as pltpu

D_MODEL = 1024
PAST_LEN = 1024
CHUNK = 64
WINDOW = 128
A_HEADS = 8
A_KV_HEADS = 2
A_HEAD_DIM = 64
A_GROUP = A_HEADS // A_KV_HEADS
A_SCALE = A_HEAD_DIM ** -0.5
ROT_DIM = A_HEAD_DIM // 4
ROPE_THETA = 500000.0
A_Q = A_HEADS * A_HEAD_DIM
A_KV = A_KV_HEADS * A_HEAD_DIM
B_HEADS = 4
B_DK = 128
B_DV = 256
B_QK = B_HEADS * B_DK
B_V = B_HEADS * B_DV
GATE_RANK = 16
GATE_TAU = 16.0
M_HEADS = 4
M_HEAD_DIM = 128
M_Q = M_HEADS * M_HEAD_DIM
D_FF = 4 * D_MODEL
N_BRANCH = 3
LN_EPS = 1e-5
RMS_EPS = 1e-6
NEG_INF = -1e30

LANES = 128
GK_PAD = LANES
VMEM_LIMIT_BYTES = 60000 * 1024
GATE_BLOCK = 512
MLP_SUB = 256
MIXER_ORDER = "A M B A ga B A M B A ga M B M A A B A M gm B A M A B gm A B A gb M B A B gb B M"

_C_AQ = 0
_C_AK = _C_AQ + A_Q
_C_AV = _C_AK + A_KV
_C_GK = _C_AV + A_KV
_C_BQ = _C_GK + GK_PAD
_C_BK = _C_BQ + B_QK
_C_BV = _C_BK + B_QK
_C_BG = _C_BV + B_V
_C_MQ = _C_BG + B_V
_C_GT = _C_MQ + M_Q
_C_END = _C_GT + N_BRANCH * D_MODEL

_NT = (((1,), (1,)), ((), ()))
_TN = (((0,), (0,)), ((), ()))

bf16 = jnp.bfloat16
f32 = jnp.float32


def _dot(a, b):
    return jnp.dot(a, b, preferred_element_type=f32)


def _dot_nt(a, b):
    return lax.dot_general(a, b, _NT, preferred_element_type=f32)


def _dot_tn(a, b):
    return lax.dot_general(a, b, _TN, preferred_element_type=f32)


def _layer_norm(z, g, b):
    mu = jnp.mean(z, axis=-1, keepdims=True)
    zc = z - mu
    var = jnp.mean(zc * zc, axis=-1, keepdims=True)
    return zc * lax.rsqrt(var + LN_EPS) * g + b


def _sigmoid(z):
    return 0.5 * jnp.tanh(0.5 * z) + 0.5


def _mixer_kernel(*refs, layer, tile, chunk, keep, has_cache, mask_start, alpha):
    it = iter(refs)
    sinks_ref = next(it)
    x_ref = next(it)
    cos_ref, sin_lo_ref, sin_hi_ref = next(it), next(it), next(it)
    mk_ref, mv_ref = next(it), next(it)
    if has_cache:
        ck_ref, cv_ref, s0_ref = next(it), next(it), next(it)
    w_in_ref, w_gk2_ref, b_gk_ref, gn_ref = next(it), next(it), next(it), next(it)
    wpa_ref, wpb_ref, wpm_ref, wout_ref = next(it), next(it), next(it), next(it)
    ln_g_ref, ln_b_ref = next(it), next(it)
    y_ref, wk_ref, wv_ref, st_ref = next(it), next(it), next(it), next(it)
    klo_s, khi_s, v_s, st_s = next(it), next(it), next(it), next(it)

    t = pl.program_id(1)
    last_t = pl.num_programs(1) - 1
    n_ch = tile // chunk
    win = keep + chunk

    lane = lax.broadcasted_iota(jnp.int32, (1, LANES), 1)
    low_half = lane < A_HEAD_DIM

    @pl.when(t == 0)
    def _init():
        if has_cache:
            ck = ck_ref[0]
            klo_s[0:keep, :] = jnp.where(low_half, ck, 0.0).astype(bf16)
            khi_s[0:keep, :] = jnp.where(low_half, 0.0, ck).astype(bf16)
            v_s[0:keep, :] = cv_ref[0].astype(bf16)
            for h in range(B_HEADS):
                st_s[h] = s0_ref[0, h].T
        else:
            zero = jnp.zeros((keep, LANES), bf16)
            klo_s[0:keep, :] = zero
            khi_s[0:keep, :] = zero
            v_s[0:keep, :] = zero
            st_s[...] = jnp.zeros(st_s.shape, f32)

    x = x_ref[0]
    xb = x.astype(bf16)
    res = {}

    def gated(key, proj):
        if "gate_" + key in res:
            res["gated_" + key] = res.pop("gate_" + key) * proj
        else:
            res["proj_" + key] = proj

    def gate_task(key, col):
        blocks = []
        n_blocks = D_MODEL // GATE_BLOCK
        for j in range(n_blocks):
            cs = slice(col + j * GATE_BLOCK, col + (j + 1) * GATE_BLOCK)
            blocks.append(_sigmoid(_dot(xb, w_in_ref[:, cs])))
            if j == n_blocks - 1:
                gate = jnp.concatenate(blocks, axis=1) if n_blocks > 1 else blocks[0]
                if "proj_" + key in res:
                    res["gated_" + key] = gate * res.pop("proj_" + key)
                else:
                    res["gate_" + key] = gate
            yield

    def task_a():
        cos = cos_ref[...]
        sin_lo = sin_lo_ref[...]
        sin_hi = sin_hi_ref[...]

        def rope(z):
            return (z * cos + pltpu.roll(z, LANES - ROT_DIM // 2, 1) * sin_lo
                    + pltpu.roll(z, ROT_DIM // 2, 1) * sin_hi)

        qkv = _dot(xb, w_in_ref[:, _C_AQ:_C_BQ])
        res["gk1"] = qkv[:, _C_GK:_C_BQ].astype(bf16)
        yield
        q_cols = [(rope(qkv[:, j * LANES:(j + 1) * LANES]) * A_SCALE).astype(bf16) for j in range(A_GROUP)]
        k_r = rope(qkv[:, _C_AK:_C_AV])
        v_new = qkv[:, _C_AV:_C_GK]
        klo_s[keep:keep + tile, :] = jnp.where(low_half, k_r, 0.0).astype(bf16)
        khi_s[keep:keep + tile, :] = jnp.where(low_half, 0.0, k_r).astype(bf16)
        v_s[keep:keep + tile, :] = v_new.astype(bf16)

        if tile >= keep:
            wk_ref[0] = k_r[tile - keep:, :]
            wv_ref[0] = v_new[tile - keep:, :]
        else:
            wk_ref[0, 0:keep - tile, :] = ck_ref[0, tile:keep, :]
            wv_ref[0, 0:keep - tile, :] = cv_ref[0, tile:keep, :]
            wk_ref[0, keep - tile:keep, :] = k_r
            wv_ref[0, keep - tile:keep, :] = v_new
        yield

        rows = lax.broadcasted_iota(jnp.int32, (A_GROUP * chunk, 1), 0)
        key_col = lax.broadcasted_iota(jnp.int32, (1, win), 1)
        sinks = []
        for g in range(A_KV_HEADS):
            sk = jnp.full((A_GROUP * chunk, 1), sinks_ref[layer, g * A_GROUP + A_GROUP - 1], f32)
            for j in range(A_GROUP - 2, -1, -1):
                sk = jnp.where(rows < (j + 1) * chunk, sinks_ref[layer, g * A_GROUP + j], sk)
            sinks.append(sk)

        def scores(c):
            q_st = jnp.concatenate([qc[c * chunk:(c + 1) * chunk, :] for qc in q_cols], axis=0)
            out = []
            for k_ref in (klo_s, khi_s):
                s = _dot_nt(q_st, k_ref[c * chunk:c * chunk + win, :])
                if mask_start:
                    first_valid = keep - (t * tile + c * chunk)
                    s = jnp.where(key_col >= first_valid, s, NEG_INF)
                out.append(s)
            return out

        def softmax(s, sk):
            m = jnp.maximum(jnp.max(s, axis=-1, keepdims=True), sk)
            p = jnp.exp(s - m)
            den = jnp.sum(p, axis=-1, keepdims=True) + jnp.exp(sk - m)
            return (p * (1.0 / den)).astype(bf16)

        def values(c, probs):
            o_groups = [_dot(p, v_s[c * chunk:c * chunk + win, :]) for p in probs]
            o_sel = jnp.where(low_half, o_groups[0], o_groups[1])
            return jnp.concatenate([o_sel[j * chunk:(j + 1) * chunk, :] for j in range(A_GROUP)], axis=1)

        oa_chunks = []
        s_next = scores(0)
        yield
        for c in range(n_ch):
            s_cur = s_next
            if c + 1 < n_ch:
                s_next = scores(c + 1)
            probs = [softmax(s, sk) for s, sk in zip(s_cur, sinks)]
            yield
            oa_chunks.append(values(c, probs))
            yield
        o_a = jnp.concatenate(oa_chunks, axis=0).astype(bf16) if n_ch > 1 else oa_chunks[0].astype(bf16)

        if tile >= keep:
            klo_s[0:keep, :] = klo_s[tile:tile + keep, :]
            khi_s[0:keep, :] = khi_s[tile:tile + keep, :]
            v_s[0:keep, :] = v_s[tile:tile + keep, :]
        gated("a", _dot(o_a, wpa_ref[...]))

    def task_m():
        mq = _dot(xb, w_in_ref[:, _C_MQ:_C_GT]).astype(bf16)
        yield
        heads = [slice(h * M_HEAD_DIM, (h + 1) * M_HEAD_DIM) for h in range(M_HEADS)]

        def scores(hs):
            return _dot_nt(mq[:, hs], mk_ref[0, :, hs]) * (M_HEAD_DIM ** -0.5)

        def softmax(s):
            m = jnp.max(s, axis=-1, keepdims=True)
            p = jnp.exp(s - m)
            return (p * (1.0 / jnp.sum(p, axis=-1, keepdims=True))).astype(bf16)

        om_heads = []
        for i in range(0, M_HEADS, 2):
            pair = heads[i:i + 2]
            s_pair = [scores(hs) for hs in pair]
            yield
            p_pair = [softmax(s) for s in s_pair]
            yield
            om_heads += [_dot(p, mv_ref[0, :, hs]) for p, hs in zip(p_pair, pair)]
            yield
        o_m = jnp.concatenate(om_heads, axis=1).astype(bf16)
        gated("m", _dot(o_m, wpm_ref[...]))

    def task_b():
        gk = _dot(res["gk1"], w_gk2_ref[...]) + b_gk_ref[...]
        log_a = (jnp.minimum(gk, 0.0) - jnp.log(1.0 + jnp.exp(-jnp.abs(gk)))) * (1.0 / GATE_TAU)
        yield
        r_i = lax.broadcasted_iota(jnp.int32, (tile, tile), 0)
        c_i = lax.broadcasted_iota(jnp.int32, (tile, tile), 1)
        if n_ch > 1:
            causal = (c_i <= r_i) & ((r_i // chunk) == (c_i // chunk))
        else:
            causal = c_i <= r_i
        tri = jnp.where(causal, 1.0, 0.0).astype(bf16)
        la_hi = log_a.astype(bf16)
        la_lo = (log_a - la_hi.astype(f32)).astype(bf16)
        cum = _dot(tri, la_hi) + _dot(tri, la_lo)
        last_rows = [cum[(c + 1) * chunk - 1:(c + 1) * chunk, :] for c in range(n_ch)]
        if n_ch > 1:
            last = jnp.concatenate([jnp.broadcast_to(r, (chunk, B_QK)) for r in last_rows], axis=0)
        else:
            last = last_rows[0]
        decay = [jnp.exp(r) for r in last_rows]
        yield
        bq = _dot(xb, w_in_ref[:, _C_BQ:_C_BK])
        q_dec = (bq * (B_DK ** -0.5) * jnp.exp(cum)).astype(bf16)
        yield
        bk = _dot(xb, w_in_ref[:, _C_BK:_C_BV])
        k_inv = (bk * jnp.exp(-cum)).astype(bf16)
        k_end = (bk * jnp.exp(last - cum)).astype(bf16)
        yield
        gn = gn_ref[...]
        ob_heads = []
        for h0 in range(0, B_HEADS, 2):
            pair = (h0, h0 + 1)
            ks = [slice(h * B_DK, (h + 1) * B_DK) for h in pair]
            vv = _dot(xb, w_in_ref[:, _C_BV + h0 * B_DV:_C_BV + (h0 + 2) * B_DV]).astype(bf16)
            vh = [vv[:, :B_DV], vv[:, B_DV:]]
            qd = [q_dec[:, s] for s in ks]
            ke = [k_end[:, s] for s in ks]
            att = [jnp.where(causal, _dot_nt(qd[i], k_inv[:, ks[i]]), 0.0).astype(bf16) for i in range(2)]
            yield
            o_intra = [_dot(att[i], vh[i]) for i in range(2)]
            chunks = [slice(c * chunk, (c + 1) * chunk) for c in range(n_ch)]
            upd = [[_dot_tn(vh[i][cs, :], ke[i][cs, :]) for i in range(2)] for cs in chunks]
            state = [st_s[h] for h in pair]
            o_inter = [[], []]
            for c, cs in enumerate(chunks):
                for i in range(2):
                    o_inter[i].append(_dot_nt(qd[i][cs, :], state[i].astype(bf16)))
                state = [state[i] * decay[c][:, ks[i]] + upd[c][i] for i in range(2)]
            for i, h in enumerate(pair):
                st_s[h] = state[i]
            yield
            gg = _dot(xb, w_in_ref[:, _C_BG + h0 * B_DV:_C_BG + (h0 + 2) * B_DV])
            for i in range(2):
                o = o_intra[i] + (jnp.concatenate(o_inter[i], axis=0) if n_ch > 1 else o_inter[i][0])
                o = o * lax.rsqrt(jnp.mean(o * o, axis=-1, keepdims=True) + RMS_EPS) * gn
                g_h = gg[:, i * B_DV:(i + 1) * B_DV]
                ob_heads.append((o * (g_h * _sigmoid(g_h))).astype(bf16))
            yield
        o_b = jnp.concatenate(ob_heads, axis=1)
        gated("b", _dot(o_b, wpb_ref[...]))

    tasks = {
        "A": task_a(), "M": task_m(), "B": task_b(),
        "ga": gate_task("a", _C_GT), "gb": gate_task("b", _C_GT + D_MODEL),
        "gm": gate_task("m", _C_GT + 2 * D_MODEL),
    }
    for name in MIXER_ORDER.split():
        next(tasks[name], None)
    for gen in tasks.values():
        for _ in gen:
            pass

    for key in "abm":
        if "gated_" + key not in res:
            res["gated_" + key] = res["gate_" + key] * res["proj_" + key]
    merged = res["gated_a"] + res["gated_b"] + res["gated_m"]

    z = alpha * x + _dot(merged.astype(bf16), wout_ref[...])
    y_ref[0] = _layer_norm(z, ln_g_ref[...], ln_b_ref[...])

    @pl.when(t == last_t)
    def _emit_state():
        for h in range(B_HEADS):
            st_ref[0, h] = st_s[h].T


def _layer_spec(stacked, layer):
    rest = stacked.shape[1:]
    index = lambda *_: (layer,) + (0,) * len(rest)
    return pl.BlockSpec((None,) + rest, index, pipeline_mode=pl.Buffered(1))


def _mixer(x, rope_tabs, mk, mv, cache, lw, layer, *, tile, chunk, alpha):
    nb, seq, _ = x.shape
    n_mem = mk.shape[2]
    keep = WINDOW
    has_cache = cache is not None
    grid = (nb, seq // tile)
    assert seq % tile == 0 and tile % chunk == 0
    assert tile >= keep or (has_cache and grid[1] == 1)

    per_b = lambda b, t: (b, 0, 0)
    layer_b = lambda b, t: (layer, b, 0, 0)
    in_specs = [
        pl.BlockSpec(memory_space=pltpu.SMEM),
        pl.BlockSpec((1, tile, D_MODEL), lambda b, t: (b, t, 0)),
        pl.BlockSpec((tile, LANES), lambda b, t: (t, 0)),
        pl.BlockSpec((tile, LANES), lambda b, t: (t, 0)),
        pl.BlockSpec((tile, LANES), lambda b, t: (t, 0)),
        pl.BlockSpec((None, 1, n_mem, M_Q), layer_b),
        pl.BlockSpec((None, 1, n_mem, M_Q), layer_b),
    ]
    args = [lw["sinks"], x, *rope_tabs, mk, mv]
    if has_cache:
        in_specs += [
            pl.BlockSpec((None, 1, keep, LANES), layer_b),
            pl.BlockSpec((None, 1, keep, LANES), layer_b),
            pl.BlockSpec((None, 1, B_HEADS, B_DK, B_DV), lambda b, t: (layer, b, 0, 0, 0)),
        ]
        args += list(cache)
    weights = [lw["w_in"], lw["w_gk2"], lw["b_gk"], lw["gn"], lw["wpa"], lw["wpb"], lw["wpm"], lw["wout"],
               lw["ln1_g"], lw["ln1_b"]]
    in_specs += [_layer_spec(w, layer) for w in weights]
    args += weights

    out_shape = (
        jax.ShapeDtypeStruct((nb, seq, D_MODEL), f32),
        jax.ShapeDtypeStruct((nb, keep, LANES), f32),
        jax.ShapeDtypeStruct((nb, keep, LANES), f32),
        jax.ShapeDtypeStruct((nb, B_HEADS, B_DK, B_DV), f32),
    )
    out_specs = (
        pl.BlockSpec((1, tile, D_MODEL), lambda b, t: (b, t, 0)),
        pl.BlockSpec((1, keep, LANES), per_b),
        pl.BlockSpec((1, keep, LANES), per_b),
        pl.BlockSpec((1, B_HEADS, B_DK, B_DV), lambda b, t: (b, 0, 0, 0)),
    )
    scratch = [
        pltpu.VMEM((keep + tile, LANES), bf16),
        pltpu.VMEM((keep + tile, LANES), bf16),
        pltpu.VMEM((keep + tile, LANES), bf16),
        pltpu.VMEM((B_HEADS, B_DV, B_DK), f32),
    ]
    kern = functools.partial(_mixer_kernel, layer=layer, tile=tile, chunk=chunk, keep=keep, has_cache=has_cache,
                             mask_start=not has_cache, alpha=alpha)
    return pl.pallas_call(
        kern,
        grid=grid,
        in_specs=in_specs,
        out_specs=out_specs,
        out_shape=out_shape,
        scratch_shapes=scratch,
        compiler_params=pltpu.CompilerParams(
            dimension_semantics=("arbitrary", "arbitrary"), vmem_limit_bytes=VMEM_LIMIT_BYTES),
        name="mixer_sample" if has_cache else "mixer_prompt",
    )(*args)


def _mlp_kernel(x_ref, wup_ref, bup_ref, wdn_ref, bdn_ref, g_ref, b_ref, y_ref, *, ff_block, sub, alpha):
    pending = None

    def finish(rows, z):
        y_ref[rows, :] = _layer_norm(z, g_ref[...], b_ref[...])

    for r in range(x_ref.shape[0] // sub):
        rows = slice(r * sub, (r + 1) * sub)
        x = x_ref[rows, :]
        xb = x.astype(bf16)
        acc = None
        for j in range(D_FF // ff_block):
            fs = slice(j * ff_block, (j + 1) * ff_block)
            h = jnp.maximum(_dot(xb, wup_ref[:, fs]) + bup_ref[:, fs], 0.0)
            part = _dot((h * h).astype(bf16), wdn_ref[fs, :])
            acc = part if acc is None else acc + part
            if j == 0 and pending is not None:
                finish(*pending)
        pending = (rows, alpha * x + (acc + bdn_ref[...]))
    finish(*pending)


def _mlp(x2d, lw, layer, *, tile, alpha):
    n = x2d.shape[0]
    assert n % tile == 0
    weights = [lw["wup"], lw["bup"], lw["wdn"], lw["bdn"], lw["ln2_g"], lw["ln2_b"]]
    return pl.pallas_call(
        functools.partial(_mlp_kernel, ff_block=1024, sub=min(tile, MLP_SUB), alpha=alpha),
        grid=(n // tile,),
        in_specs=[pl.BlockSpec((tile, D_MODEL), lambda i: (i, 0))] + [_layer_spec(w, layer) for w in weights],
        out_specs=pl.BlockSpec((tile, D_MODEL), lambda i: (i, 0)),
        out_shape=jax.ShapeDtypeStruct((n, D_MODEL), f32),
        compiler_params=pltpu.CompilerParams(
            dimension_semantics=("arbitrary",), vmem_limit_bytes=VMEM_LIMIT_BYTES),
        name="mlp",
    )(x2d, *weights)


def _memkv_kernel(mem_ref, w_ref, k_ref, v_ref, kb_ref, vb_ref):
    rows, n_mem, _ = mem_ref.shape
    kv = _dot(mem_ref[...].reshape(rows * n_mem, D_MODEL).astype(bf16), w_ref[...])
    k = kv[:, :M_Q].reshape(rows, n_mem, M_Q)
    v = kv[:, M_Q:].reshape(rows, n_mem, M_Q)
    k_ref[...] = k
    v_ref[...] = v
    kb_ref[...] = k.astype(bf16)
    vb_ref[...] = v.astype(bf16)


def _memkv(mem, w_mem_kv_b):
    depth = w_mem_kv_b.shape[0]
    nb, n_mem, _ = mem.shape
    rows = _pick_tile(nb, 4)
    out_spec = pl.BlockSpec((None, rows, n_mem, M_Q), lambda l, b: (l, b, 0, 0))
    return pl.pallas_call(
        _memkv_kernel,
        grid=(depth, nb // rows),
        in_specs=[pl.BlockSpec((rows, n_mem, D_MODEL), lambda l, b: (b, 0, 0)),
                  pl.BlockSpec((None, D_MODEL, 2 * M_Q), lambda l, b: (l, 0, 0))],
        out_specs=(out_spec,) * 4,
        out_shape=(jax.ShapeDtypeStruct((depth, nb, n_mem, M_Q), f32),) * 2
        + (jax.ShapeDtypeStruct((depth, nb, n_mem, M_Q), bf16),) * 2,
        compiler_params=pltpu.CompilerParams(
            dimension_semantics=("arbitrary", "arbitrary"), vmem_limit_bytes=VMEM_LIMIT_BYTES),
        name="memkv",
    )(mem, w_mem_kv_b)


def _rope_tables(pos):
    half = ROT_DIM // 2
    inv = ROPE_THETA ** (-jnp.arange(half, dtype=f32) / half)
    ang = pos.astype(f32)[:, None] * inv[None, :]
    cos, sin = jnp.cos(ang), jnp.sin(ang)
    n = pos.shape[0]
    pad = jnp.zeros((n, A_HEAD_DIM - ROT_DIM), f32)
    zero = jnp.zeros((n, half), f32)
    cos_h = jnp.concatenate([cos, cos, jnp.ones_like(pad)], axis=1)
    lo_h = jnp.concatenate([-sin, zero, pad], axis=1)
    hi_h = jnp.concatenate([zero, sin, pad], axis=1)
    return tuple(jnp.concatenate([t_, t_], axis=1) for t_ in (cos_h, lo_h, hi_h))


def _prep_weights(w_in, w_gk2, b_gk, attn_sinks, gla_norm_g, w_proj_a, w_proj_b, w_proj_m, w_out,
                  ln1_g, ln1_b, w_up, b_up, w_down, b_down, ln2_g, ln2_b):
    wb = w_in.astype(bf16)
    col = lambda lo, hi: wb[:, :, lo:hi]
    src_bq = A_Q + 2 * A_KV
    src_gk = src_bq + 2 * B_QK + 2 * B_V
    src_mq = src_gk + GATE_RANK
    head_order = [h for j in range(A_GROUP) for h in (j, j + A_GROUP)]
    aq = [col(h * A_HEAD_DIM, (h + 1) * A_HEAD_DIM) for h in head_order]
    gk = jnp.pad(col(src_gk, src_mq), ((0, 0), (0, 0), (0, GK_PAD - GATE_RANK)))
    w_in_r = jnp.concatenate(aq + [col(A_Q, src_bq), gk, col(src_bq, src_gk), col(src_mq, wb.shape[2])], axis=-1)
    assert w_in_r.shape[2] == _C_END
    wpa = jnp.concatenate([w_proj_a[:, h * A_HEAD_DIM:(h + 1) * A_HEAD_DIM, :] for h in head_order], axis=1)
    row = lambda v: v[:, None, :]
    return dict(
        sinks=attn_sinks,
        w_in=w_in_r,
        w_gk2=jnp.pad(w_gk2, ((0, 0), (0, GK_PAD - GATE_RANK), (0, 0))).astype(bf16),
        b_gk=row(b_gk), gn=row(gla_norm_g),
        wpa=wpa.astype(bf16), wpb=w_proj_b.astype(bf16), wpm=w_proj_m.astype(bf16), wout=w_out.astype(bf16),
        ln1_g=row(ln1_g), ln1_b=row(ln1_b),
        wup=w_up.astype(bf16), bup=row(b_up), wdn=w_down.astype(bf16), bdn=row(b_down),
        ln2_g=row(ln2_g), ln2_b=row(ln2_b),
    )


def _pick_tile(n, pref):
    tile = min(pref, n)
    while n % tile:
        tile //= 2
    return tile


def kernel(x_prompt, x_sample, cache_win_k, cache_win_v, state_gla, cache_mem_k, cache_mem_v, mem_prompt,
           w_in, w_gk2, b_gk, attn_sinks, gla_norm_g, w_mem_kv, w_proj_a, w_proj_b, w_proj_m, w_out,
           ln1_g, ln1_b, w_up, b_up, w_down, b_down, ln2_g, ln2_b):
    depth = w_in.shape[0]
    alpha = (2 * depth) ** 0.25
    lws = _prep_weights(w_in, w_gk2, b_gk, attn_sinks, gla_norm_g, w_proj_a, w_proj_b, w_proj_m, w_out,
                        ln1_g, ln1_b, w_up, b_up, w_down, b_down, ln2_g, ln2_b)

    bp, sp, _ = x_prompt.shape
    n_mem = mem_prompt.shape[1]
    mk_p, mv_p, mk_b, mv_b = _memkv(mem_prompt, w_mem_kv.astype(bf16))
    rope_p = _rope_tables(jnp.arange(sp, dtype=jnp.int32))
    mix_tile = _pick_tile(sp, 256)
    mlp_tile = _pick_tile(bp * sp, 1024)
    x = x_prompt
    wk_p, wv_p, gs_p = [], [], []
    for l in range(depth):
        x, wk, wv, st = _mixer(x, rope_p, mk_b, mv_b, None, lws, l, tile=mix_tile, chunk=CHUNK, alpha=alpha)
        x = _mlp(x.reshape(bp * sp, D_MODEL), lws, l, tile=mlp_tile, alpha=alpha).reshape(bp, sp, D_MODEL)
        wk_p.append(wk)
        wv_p.append(wv)
        gs_p.append(st)
    y_prompt = x

    bs, ts, _ = x_sample.shape
    keep = cache_win_k.shape[2]
    rope_s = _rope_tables(PAST_LEN + jnp.arange(ts, dtype=jnp.int32))
    cache = (cache_win_k.reshape(depth, bs, keep, A_KV), cache_win_v.reshape(depth, bs, keep, A_KV), state_gla)
    mk_s = cache_mem_k.reshape(depth, bs, n_mem, M_Q).astype(bf16)
    mv_s = cache_mem_v.reshape(depth, bs, n_mem, M_Q).astype(bf16)
    x = x_sample
    wk_s, wv_s, gs_s = [], [], []
    for l in range(depth):
        x, wk, wv, st = _mixer(x, rope_s, mk_s, mv_s, cache, lws, l, tile=ts, chunk=ts, alpha=alpha)
        x = _mlp(x.reshape(bs * ts, D_MODEL), lws, l, tile=_pick_tile(bs * ts, 512), alpha=alpha)
        x = x.reshape(bs, ts, D_MODEL)
        wk_s.append(wk)
        wv_s.append(wv)
        gs_s.append(st)
    y_sample = x

    kv_shape = lambda b: (depth, b, keep, A_KV_HEADS, A_HEAD_DIM)
    mem_shape = (depth, bp, n_mem, M_HEADS, M_HEAD_DIM)
    return (y_prompt, y_sample,
            jnp.stack(wk_p).reshape(kv_shape(bp)), jnp.stack(wv_p).reshape(kv_shape(bp)), jnp.stack(gs_p),
            mk_p.reshape(mem_shape), mv_p.reshape(mem_shape),
            jnp.stack(wk_s).reshape(kv_shape(bs)), jnp.stack(wv_s).reshape(kv_shape(bs)), jnp.stack(gs_s))
```

```python
import functools

import jax
import jax.numpy as jnp
from jax import lax
from jax.experimental import pallas as pl
from jax.experimental.pallas import tpu as pltpu

D_MODEL = 1024
PAST_LEN = 1024
CHUNK = 64
WINDOW = 128
A_HEADS = 8
A_KV_HEADS = 2
A_HEAD_DIM = 64
A_GROUP = A_HEADS // A_KV_HEADS
A_SCALE = A_HEAD_DIM ** -0.5
ROT_DIM = A_HEAD_DIM // 4
ROPE_THETA = 500000.0
A_Q = A_HEADS * A_HEAD_DIM
A_KV = A_KV_HEADS * A_HEAD_DIM
B_HEADS = 4
B_DK = 128
B_DV = 256
B_QK = B_HEADS * B_DK
B_V = B_HEADS * B_DV
GATE_RANK = 16
GATE_TAU = 16.0
M_HEADS = 4
M_HEAD_DIM = 128
M_Q = M_HEADS * M_HEAD_DIM
D_FF = 4 * D_MODEL
N_BRANCH = 3
LN_EPS = 1e-5
RMS_EPS = 1e-6
NEG_INF = -1e30

LANES = 128
GK_PAD = LANES
VMEM_LIMIT_BYTES = 60000 * 1024
GATE_BLOCK = 512
MLP_SUB = 256
GLA_SUB = 256
MIX_TILE = 512
MIXER_ORDER = ("A M B A ga B A M B A ga A M A B A M A gm A M A B A M A gm A M A B A gb A B A M A B A gb"
               " B A B B")

_C_AQ = 0
_C_AK = _C_AQ + A_Q
_C_AV = _C_AK + A_KV
_C_GK = _C_AV + A_KV
_C_BQ = _C_GK + GK_PAD
_C_BK = _C_BQ + B_QK
_C_BV = _C_BK + B_QK
_C_BG = _C_BV + B_V
_C_MQ = _C_BG + B_V
_C_GT = _C_MQ + M_Q
_C_END = _C_GT + N_BRANCH * D_MODEL

_NT = (((1,), (1,)), ((), ()))
_TN = (((0,), (0,)), ((), ()))

bf16 = jnp.bfloat16
f32 = jnp.float32


def _dot(a, b):
    return jnp.dot(a, b, preferred_element_type=f32)


def _dot_nt(a, b):
    return lax.dot_general(a, b, _NT, preferred_element_type=f32)


def _dot_tn(a, b):
    return lax.dot_general(a, b, _TN, preferred_element_type=f32)


def _layer_norm(z, g, b):
    mu = jnp.mean(z, axis=-1, keepdims=True)
    zc = z - mu
    var = jnp.mean(zc * zc, axis=-1, keepdims=True)
    return zc * lax.rsqrt(var + LN_EPS) * g + b


def _sigmoid(z):
    return 0.5 * jnp.tanh(0.5 * z) + 0.5


def _mixer_kernel(*refs, layer, tile, chunk, keep, has_cache, mask_start, alpha):
    it = iter(refs)
    sinks_ref = next(it)
    x_ref = next(it)
    cos_ref, sin_lo_ref, sin_hi_ref = next(it), next(it), next(it)
    mk_ref, mv_ref = next(it), next(it)
    if has_cache:
        ck_ref, cv_ref, s0_ref = next(it), next(it), next(it)
    w_in_ref, w_gk2_ref, b_gk_ref, gn_ref = next(it), next(it), next(it), next(it)
    wpa_ref, wpb_ref, wpm_ref, wout_ref = next(it), next(it), next(it), next(it)
    ln_g_ref, ln_b_ref = next(it), next(it)
    y_ref, wk_ref, wv_ref, st_ref = next(it), next(it), next(it), next(it)
    klo_s, khi_s, v_s, st_s = next(it), next(it), next(it), next(it)

    t = pl.program_id(1)
    last_t = pl.num_programs(1) - 1
    n_ch = tile // chunk
    win = keep + chunk

    lane = lax.broadcasted_iota(jnp.int32, (1, LANES), 1)
    low_half = lane < A_HEAD_DIM

    @pl.when(t == 0)
    def _init():
        if has_cache:
            ck = ck_ref[0]
            klo_s[0:keep, :] = jnp.where(low_half, ck, 0.0).astype(bf16)
            khi_s[0:keep, :] = jnp.where(low_half, 0.0, ck).astype(bf16)
            v_s[0:keep, :] = cv_ref[0].astype(bf16)
            for h in range(B_HEADS):
                st_s[h] = s0_ref[0, h].T
        else:
            zero = jnp.zeros((keep, LANES), bf16)
            klo_s[0:keep, :] = zero
            khi_s[0:keep, :] = zero
            v_s[0:keep, :] = zero
            st_s[...] = jnp.zeros(st_s.shape, f32)

    x = x_ref[0]
    xb = x.astype(bf16)
    res = {}

    def gated(key, proj):
        if "gate_" + key in res:
            res["gated_" + key] = res.pop("gate_" + key) * proj
        else:
            res["proj_" + key] = proj

    def gate_task(key, col):
        blocks = []
        n_blocks = D_MODEL // GATE_BLOCK
        for j in range(n_blocks):
            cs = slice(col + j * GATE_BLOCK, col + (j + 1) * GATE_BLOCK)
            blocks.append(_sigmoid(_dot(xb, w_in_ref[:, cs])))
            if j == n_blocks - 1:
                gate = jnp.concatenate(blocks, axis=1) if n_blocks > 1 else blocks[0]
                if "proj_" + key in res:
                    res["gated_" + key] = gate * res.pop("proj_" + key)
                else:
                    res["gate_" + key] = gate
            yield

    def task_a():
        cos = cos_ref[...]
        sin_lo = sin_lo_ref[...]
        sin_hi = sin_hi_ref[...]

        def rope(z):
            return (z * cos + pltpu.roll(z, LANES - ROT_DIM // 2, 1) * sin_lo
                    + pltpu.roll(z, ROT_DIM // 2, 1) * sin_hi)

        qkv = _dot(xb, w_in_ref[:, _C_AQ:_C_BQ])
        res["gk1"] = qkv[:, _C_GK:_C_BQ].astype(bf16)
        yield
        q_cols = [(rope(qkv[:, j * LANES:(j + 1) * LANES]) * A_SCALE).astype(bf16) for j in range(A_GROUP)]
        k_r = rope(qkv[:, _C_AK:_C_AV])
        v_new = qkv[:, _C_AV:_C_GK]
        klo_s[keep:keep + tile, :] = jnp.where(low_half, k_r, 0.0).astype(bf16)
        khi_s[keep:keep + tile, :] = jnp.where(low_half, 0.0, k_r).astype(bf16)
        v_s[keep:keep + tile, :] = v_new.astype(bf16)

        if tile >= keep:
            wk_ref[0] = k_r[tile - keep:, :]
            wv_ref[0] = v_new[tile - keep:, :]
        else:
            wk_ref[0, 0:keep - tile, :] = ck_ref[0, tile:keep, :]
            wv_ref[0, 0:keep - tile, :] = cv_ref[0, tile:keep, :]
            wk_ref[0, keep - tile:keep, :] = k_r
            wv_ref[0, keep - tile:keep, :] = v_new
        yield

        rows = lax.broadcasted_iota(jnp.int32, (A_GROUP * chunk, 1), 0)
        key_col = lax.broadcasted_iota(jnp.int32, (1, win), 1)
        sinks = []
        for g in range(A_KV_HEADS):
            sk = jnp.full((A_GROUP * chunk, 1), sinks_ref[layer, g * A_GROUP + A_GROUP - 1], f32)
            for j in range(A_GROUP - 2, -1, -1):
                sk = jnp.where(rows < (j + 1) * chunk, sinks_ref[layer, g * A_GROUP + j], sk)
            sinks.append(sk)

        def scores(c):
            q_st = jnp.concatenate([qc[c * chunk:(c + 1) * chunk, :] for qc in q_cols], axis=0)
            out = []
            for k_ref in (klo_s, khi_s):
                s = _dot_nt(q_st, k_ref[c * chunk:c * chunk + win, :])
                if mask_start:
                    first_valid = keep - (t * tile + c * chunk)
                    s = jnp.where(key_col >= first_valid, s, NEG_INF)
                out.append(s)
            return out

        def softmax(s, sk):
            m = jnp.maximum(jnp.max(s, axis=-1, keepdims=True), sk)
            p = jnp.exp(s - m)
            den = jnp.sum(p, axis=-1, keepdims=True) + jnp.exp(sk - m)
            return (p * (1.0 / den)).astype(bf16)

        def values(c, probs):
            o_groups = [_dot(p, v_s[c * chunk:c * chunk + win, :]) for p in probs]
            o_sel = jnp.where(low_half, o_groups[0], o_groups[1])
            return jnp.concatenate([o_sel[j * chunk:(j + 1) * chunk, :] for j in range(A_GROUP)], axis=1)

        oa_chunks = []
        s_next = scores(0)
        yield
        for c in range(n_ch):
            s_cur = s_next
            if c + 1 < n_ch:
                s_next = scores(c + 1)
            probs = [softmax(s, sk) for s, sk in zip(s_cur, sinks)]
            yield
            oa_chunks.append(values(c, probs))
            yield
        o_a = jnp.concatenate(oa_chunks, axis=0).astype(bf16) if n_ch > 1 else oa_chunks[0].astype(bf16)

        if tile >= keep:
            klo_s[0:keep, :] = klo_s[tile:tile + keep, :]
            khi_s[0:keep, :] = khi_s[tile:tile + keep, :]
            v_s[0:keep, :] = v_s[tile:tile + keep, :]
        gated("a", _dot(o_a, wpa_ref[...]))

    def task_m():
        mq = _dot(xb, w_in_ref[:, _C_MQ:_C_GT]).astype(bf16)
        yield
        heads = [slice(h * M_HEAD_DIM, (h + 1) * M_HEAD_DIM) for h in range(M_HEADS)]

        def scores(hs):
            return _dot_nt(mq[:, hs], mk_ref[0, :, hs]) * (M_HEAD_DIM ** -0.5)

        def softmax(s):
            m = jnp.max(s, axis=-1, keepdims=True)
            p = jnp.exp(s - m)
            return (p * (1.0 / jnp.sum(p, axis=-1, keepdims=True))).astype(bf16)

        om_heads = []
        for i in range(0, M_HEADS, 2):
            pair = heads[i:i + 2]
            s_pair = [scores(hs) for hs in pair]
            yield
            p_pair = [softmax(s) for s in s_pair]
            yield
            om_heads += [_dot(p, mv_ref[0, :, hs]) for p, hs in zip(p_pair, pair)]
            yield
        o_m = jnp.concatenate(om_heads, axis=1).astype(bf16)
        gated("m", _dot(o_m, wpm_ref[...]))

    def task_b():
        gk = _dot(res["gk1"], w_gk2_ref[...]) + b_gk_ref[...]
        log_a = (jnp.minimum(gk, 0.0) - jnp.log(1.0 + jnp.exp(-jnp.abs(gk)))) * (1.0 / GATE_TAU)
        yield
        gsub = min(tile, GLA_SUB)
        subs = [slice(s * gsub, (s + 1) * gsub) for s in range(tile // gsub)]
        r_i = lax.broadcasted_iota(jnp.int32, (gsub, gsub), 0)
        c_i = lax.broadcasted_iota(jnp.int32, (gsub, gsub), 1)
        if gsub > chunk:
            causal = (c_i <= r_i) & ((r_i // chunk) == (c_i // chunk))
        else:
            causal = c_i <= r_i
        tri = jnp.where(causal, 1.0, 0.0).astype(bf16)
        la_hi = log_a.astype(bf16)
        la_lo = (log_a - la_hi.astype(f32)).astype(bf16)
        cum = [_dot(tri, la_hi[rs, :]) + _dot(tri, la_lo[rs, :]) for rs in subs]
        cum = jnp.concatenate(cum, axis=0) if len(subs) > 1 else cum[0]
        last_rows = [cum[(c + 1) * chunk - 1:(c + 1) * chunk, :] for c in range(n_ch)]
        if n_ch > 1:
            last = jnp.concatenate([jnp.broadcast_to(r, (chunk, B_QK)) for r in last_rows], axis=0)
        else:
            last = last_rows[0]
        decay = [jnp.exp(r) for r in last_rows]
        yield
        bq = _dot(xb, w_in_ref[:, _C_BQ:_C_BK])
        q_dec = (bq * (B_DK ** -0.5) * jnp.exp(cum)).astype(bf16)
        yield
        bk = _dot(xb, w_in_ref[:, _C_BK:_C_BV])
        k_inv = (bk * jnp.exp(-cum)).astype(bf16)
        k_end = (bk * jnp.exp(last - cum)).astype(bf16)
        yield
        gn = gn_ref[...]
        ob_heads = []
        for h0 in range(0, B_HEADS, 2):
            pair = (h0, h0 + 1)
            ks = [slice(h * B_DK, (h + 1) * B_DK) for h in pair]
            vv = _dot(xb, w_in_ref[:, _C_BV + h0 * B_DV:_C_BV + (h0 + 2) * B_DV]).astype(bf16)
            vh = [vv[:, :B_DV], vv[:, B_DV:]]
            qd = [q_dec[:, s] for s in ks]
            ke = [k_end[:, s] for s in ks]
            att = [[jnp.where(causal, _dot_nt(qd[i][rs, :], k_inv[rs, ks[i]]), 0.0).astype(bf16) for i in range(2)]
                   for rs in subs]
            yield
            o_intra = [[_dot(att[s][i], vh[i][rs, :]) for i in range(2)] for s, rs in enumerate(subs)]
            o_intra = [jnp.concatenate([o[i] for o in o_intra], axis=0) if len(subs) > 1 else o_intra[0][i]
                       for i in range(2)]
            chunks = [slice(c * chunk, (c + 1) * chunk) for c in range(n_ch)]
            upd = [[_dot_tn(vh[i][cs, :], ke[i][cs, :]) for i in range(2)] for cs in chunks]
            state = [st_s[h] for h in pair]
            o_inter = [[], []]
            for c, cs in enumerate(chunks):
                for i in range(2):
                    o_inter[i].append(_dot_nt(qd[i][cs, :], state[i].astype(bf16)))
                state = [state[i] * decay[c][:, ks[i]] + upd[c][i] for i in range(2)]
            for i, h in enumerate(pair):
                st_s[h] = state[i]
            yield
            gg = _dot(xb, w_in_ref[:, _C_BG + h0 * B_DV:_C_BG + (h0 + 2) * B_DV])
            for i in range(2):
                o = o_intra[i] + (jnp.concatenate(o_inter[i], axis=0) if n_ch > 1 else o_inter[i][0])
                o = o * lax.rsqrt(jnp.mean(o * o, axis=-1, keepdims=True) + RMS_EPS) * gn
                g_h = gg[:, i * B_DV:(i + 1) * B_DV]
                ob_heads.append((o * (g_h * _sigmoid(g_h))).astype(bf16))
            yield
        o_b = jnp.concatenate(ob_heads, axis=1)
        gated("b", _dot(o_b, wpb_ref[...]))

    tasks = {
        "A": task_a(), "M": task_m(), "B": task_b(),
        "ga": gate_task("a", _C_GT), "gb": gate_task("b", _C_GT + D_MODEL),
        "gm": gate_task("m", _C_GT + 2 * D_MODEL),
    }
    for name in MIXER_ORDER.split():
        next(tasks[name], None)
    for gen in tasks.values():
        for _ in gen:
            pass

    for key in "abm":
        if "gated_" + key not in res:
            res["gated_" + key] = res["gate_" + key] * res["proj_" + key]
    merged = res["gated_a"] + res["gated_b"] + res["gated_m"]

    z = alpha * x + _dot(merged.astype(bf16), wout_ref[...])
    y_ref[0] = _layer_norm(z, ln_g_ref[...], ln_b_ref[...])

    @pl.when(t == last_t)
    def _emit_state():
        for h in range(B_HEADS):
            st_ref[0, h] = st_s[h].T


def _layer_spec(stacked, layer):
    rest = stacked.shape[1:]
    index = lambda *_: (layer,) + (0,) * len(rest)
    return pl.BlockSpec((None,) + rest, index, pipeline_mode=pl.Buffered(1))


def _mixer(x, rope_tabs, mk, mv, cache, lw, layer, *, tile, chunk, alpha):
    nb, seq, _ = x.shape
    n_mem = mk.shape[2]
    keep = WINDOW
    has_cache = cache is not None
    grid = (nb, seq // tile)
    assert seq % tile == 0 and tile % chunk == 0
    assert tile >= keep or (has_cache and grid[1] == 1)

    per_b = lambda b, t: (b, 0, 0)
    layer_b = lambda b, t: (layer, b, 0, 0)
    in_specs = [
        pl.BlockSpec(memory_space=pltpu.SMEM),
        pl.BlockSpec((1, tile, D_MODEL), lambda b, t: (b, t, 0)),
        pl.BlockSpec((tile, LANES), lambda b, t: (t, 0)),
        pl.BlockSpec((tile, LANES), lambda b, t: (t, 0)),
        pl.BlockSpec((tile, LANES), lambda b, t: (t, 0)),
        pl.BlockSpec((None, 1, n_mem, M_Q), layer_b),
        pl.BlockSpec((None, 1, n_mem, M_Q), layer_b),
    ]
    args = [lw["sinks"], x, *rope_tabs, mk, mv]
    if has_cache:
        in_specs += [
            pl.BlockSpec((None, 1, keep, LANES), layer_b),
            pl.BlockSpec((None, 1, keep, LANES), layer_b),
            pl.BlockSpec((None, 1, B_HEADS, B_DK, B_DV), lambda b, t: (layer, b, 0, 0, 0)),
        ]
        args += list(cache)
    weights = [lw["w_in"], lw["w_gk2"], lw["b_gk"], lw["gn"], lw["wpa"], lw["wpb"], lw["wpm"], lw["wout"],
               lw["ln1_g"], lw["ln1_b"]]
    in_specs += [_layer_spec(w, layer) for w in weights]
    args += weights

    out_shape = (
        jax.ShapeDtypeStruct((nb, seq, D_MODEL), f32),
        jax.ShapeDtypeStruct((nb, keep, LANES), f32),
        jax.ShapeDtypeStruct((nb, keep, LANES), f32),
        jax.ShapeDtypeStruct((nb, B_HEADS, B_DK, B_DV), f32),
    )
    out_specs = (
        pl.BlockSpec((1, tile, D_MODEL), lambda b, t: (b, t, 0)),
        pl.BlockSpec((1, keep, LANES), per_b),
        pl.BlockSpec((1, keep, LANES), per_b),
        pl.BlockSpec((1, B_HEADS, B_DK, B_DV), lambda b, t: (b, 0, 0, 0)),
    )
    scratch = [
        pltpu.VMEM((keep + tile, LANES), bf16),
        pltpu.VMEM((keep + tile, LANES), bf16),
        pltpu.VMEM((keep + tile, LANES), bf16),
        pltpu.VMEM((B_HEADS, B_DV, B_DK), f32),
    ]
    kern = functools.partial(_mixer_kernel, layer=layer, tile=tile, chunk=chunk, keep=keep, has_cache=has_cache,
                             mask_start=not has_cache, alpha=alpha)
    return pl.pallas_call(
        kern,
        grid=grid,
        in_specs=in_specs,
        out_specs=out_specs,
        out_shape=out_shape,
        scratch_shapes=scratch,
        compiler_params=pltpu.CompilerParams(
            dimension_semantics=("arbitrary", "arbitrary"), vmem_limit_bytes=VMEM_LIMIT_BYTES),
        name="mixer_sample" if has_cache else "mixer_prompt",
    )(*args)


def _mlp_kernel(x_ref, wup_ref, bup_ref, wdn_ref, bdn_ref, g_ref, b_ref, y_ref, *, ff_block, sub, alpha):
    pending = None

    def finish(rows, z):
        y_ref[rows, :] = _layer_norm(z, g_ref[...], b_ref[...])

    for r in range(x_ref.shape[0] // sub):
        rows = slice(r * sub, (r + 1) * sub)
        x = x_ref[rows, :]
        xb = x.astype(bf16)
        acc = None
        for j in range(D_FF // ff_block):
            fs = slice(j * ff_block, (j + 1) * ff_block)
            h = jnp.maximum(_dot(xb, wup_ref[:, fs]) + bup_ref[:, fs], 0.0)
            part = _dot((h * h).astype(bf16), wdn_ref[fs, :])
            acc = part if acc is None else acc + part
            if j == 0 and pending is not None:
                finish(*pending)
        pending = (rows, alpha * x + (acc + bdn_ref[...]))
    finish(*pending)


def _mlp(x2d, lw, layer, *, tile, alpha):
    n = x2d.shape[0]
    assert n % tile == 0
    weights = [lw["wup"], lw["bup"], lw["wdn"], lw["bdn"], lw["ln2_g"], lw["ln2_b"]]
    return pl.pallas_call(
        functools.partial(_mlp_kernel, ff_block=1024, sub=min(tile, MLP_SUB), alpha=alpha),
        grid=(n // tile,),
        in_specs=[pl.BlockSpec((tile, D_MODEL), lambda i: (i, 0))] + [_layer_spec(w, layer) for w in weights],
        out_specs=pl.BlockSpec((tile, D_MODEL), lambda i: (i, 0)),
        out_shape=jax.ShapeDtypeStruct((n, D_MODEL), f32),
        compiler_params=pltpu.CompilerParams(
            dimension_semantics=("arbitrary",), vmem_limit_bytes=VMEM_LIMIT_BYTES),
        name="mlp",
    )(x2d, *weights)


def _memkv_kernel(mem_ref, w_ref, k_ref, v_ref, kb_ref, vb_ref):
    rows, n_mem, _ = mem_ref.shape
    kv = _dot(mem_ref[...].reshape(rows * n_mem, D_MODEL).astype(bf16), w_ref[...])
    k = kv[:, :M_Q].reshape(rows, n_mem, M_Q)
    v = kv[:, M_Q:].reshape(rows, n_mem, M_Q)
    k_ref[...] = k
    v_ref[...] = v
    kb_ref[...] = k.astype(bf16)
    vb_ref[...] = v.astype(bf16)


def _memkv(mem, w_mem_kv_b):
    depth = w_mem_kv_b.shape[0]
    nb, n_mem, _ = mem.shape
    rows = _pick_tile(nb, 4)
    out_spec = pl.BlockSpec((None, rows, n_mem, M_Q), lambda l, b: (l, b, 0, 0))
    return pl.pallas_call(
        _memkv_kernel,
        grid=(depth, nb // rows),
        in_specs=[pl.BlockSpec((rows, n_mem, D_MODEL), lambda l, b: (b, 0, 0)),
                  pl.BlockSpec((None, D_MODEL, 2 * M_Q), lambda l, b: (l, 0, 0))],
        out_specs=(out_spec,) * 4,
        out_shape=(jax.ShapeDtypeStruct((depth, nb, n_mem, M_Q), f32),) * 2
        + (jax.ShapeDtypeStruct((depth, nb, n_mem, M_Q), bf16),) * 2,
        compiler_params=pltpu.CompilerParams(
            dimension_semantics=("arbitrary", "arbitrary"), vmem_limit_bytes=VMEM_LIMIT_BYTES),
        name="memkv",
    )(mem, w_mem_kv_b)


def _rope_tables(pos):
    half = ROT_DIM // 2
    inv = ROPE_THETA ** (-jnp.arange(half, dtype=f32) / half)
    ang = pos.astype(f32)[:, None] * inv[None, :]
    cos, sin = jnp.cos(ang), jnp.sin(ang)
    n = pos.shape[0]
    pad = jnp.zeros((n, A_HEAD_DIM - ROT_DIM), f32)
    zero = jnp.zeros((n, half), f32)
    cos_h = jnp.concatenate([cos, cos, jnp.ones_like(pad)], axis=1)
    lo_h = jnp.concatenate([-sin, zero, pad], axis=1)
    hi_h = jnp.concatenate([zero, sin, pad], axis=1)
    return tuple(jnp.concatenate([t_, t_], axis=1) for t_ in (cos_h, lo_h, hi_h))


_HEAD_ORDER = [h for j in range(A_GROUP) for h in (j, j + A_GROUP)]
_S_BQ = A_Q + 2 * A_KV
_S_GK = _S_BQ + 2 * B_QK + 2 * B_V
_S_MQ = _S_GK + GATE_RANK
_S_END = _S_MQ + M_Q + N_BRANCH * D_MODEL


def _relayout_kernel(src_ref, dst_ref):
    def move(dst_lo, src_lo, width):
        dst_ref[:, dst_lo:dst_lo + width] = src_ref[:, src_lo:src_lo + width].astype(bf16)

    for k, h in enumerate(_HEAD_ORDER):
        move(_C_AQ + k * A_HEAD_DIM, h * A_HEAD_DIM, A_HEAD_DIM)
    move(_C_AK, A_Q, 2 * A_KV)
    dst_ref[:, _C_GK:_C_BQ] = jnp.zeros((dst_ref.shape[0], GK_PAD), bf16)
    move(_C_GK, _S_GK, GATE_RANK)
    move(_C_BQ, _S_BQ, _S_GK - _S_BQ)
    move(_C_MQ, _S_MQ, _S_END - _S_MQ)


def _relayout_w_in(w_in):
    depth, d_model, d_in = w_in.shape
    assert d_in == _S_END
    rows = _pick_tile(d_model, 128)
    return pl.pallas_call(
        _relayout_kernel,
        grid=(depth, d_model // rows),
        in_specs=[pl.BlockSpec((None, rows, d_in), lambda l, r: (l, r, 0))],
        out_specs=pl.BlockSpec((None, rows, _C_END), lambda l, r: (l, r, 0)),
        out_shape=jax.ShapeDtypeStruct((depth, d_model, _C_END), bf16),
        compiler_params=pltpu.CompilerParams(
            dimension_semantics=("arbitrary", "arbitrary"), vmem_limit_bytes=VMEM_LIMIT_BYTES),
        name="relayout_w_in",
    )(w_in)


def _prep_weights(w_in, w_gk2, b_gk, attn_sinks, gla_norm_g, w_proj_a, w_proj_b, w_proj_m, w_out,
                  ln1_g, ln1_b, w_up, b_up, w_down, b_down, ln2_g, ln2_b):
    head_order = _HEAD_ORDER
    w_in_r = _relayout_w_in(w_in)
    wpa = jnp.concatenate([w_proj_a[:, h * A_HEAD_DIM:(h + 1) * A_HEAD_DIM, :] for h in head_order], axis=1)
    row = lambda v: v[:, None, :]
    return dict(
        sinks=attn_sinks,
        w_in=w_in_r,
        w_gk2=jnp.pad(w_gk2, ((0, 0), (0, GK_PAD - GATE_RANK), (0, 0))).astype(bf16),
        b_gk=row(b_gk), gn=row(gla_norm_g),
        wpa=wpa.astype(bf16), wpb=w_proj_b.astype(bf16), wpm=w_proj_m.astype(bf16), wout=w_out.astype(bf16),
        ln1_g=row(ln1_g), ln1_b=row(ln1_b),
        wup=w_up.astype(bf16), bup=row(b_up), wdn=w_down.astype(bf16), bdn=row(b_down),
        ln2_g=row(ln2_g), ln2_b=row(ln2_b),
    )


def _pick_tile(n, pref):
    tile = min(pref, n)
    while n % tile:
        tile //= 2
    return tile


def kernel(x_prompt, x_sample, cache_win_k, cache_win_v, state_gla, cache_mem_k, cache_mem_v, mem_prompt,
           w_in, w_gk2, b_gk, attn_sinks, gla_norm_g, w_mem_kv, w_proj_a, w_proj_b, w_proj_m, w_out,
           ln1_g, ln1_b, w_up, b_up, w_down, b_down, ln2_g, ln2_b):
    depth = w_in.shape[0]
    alpha = (2 * depth) ** 0.25
    lws = _prep_weights(w_in, w_gk2, b_gk, attn_sinks, gla_norm_g, w_proj_a, w_proj_b, w_proj_m, w_out,
                        ln1_g, ln1_b, w_up, b_up, w_down, b_down, ln2_g, ln2_b)

    bp, sp, _ = x_prompt.shape
    n_mem = mem_prompt.shape[1]
    mk_p, mv_p, mk_b, mv_b = _memkv(mem_prompt, w_mem_kv.astype(bf16))
    rope_p = _rope_tables(jnp.arange(sp, dtype=jnp.int32))
    mix_tile = _pick_tile(sp, MIX_TILE)
    mlp_tile = _pick_tile(bp * sp, 1024)
    x = x_prompt
    wk_p, wv_p, gs_p = [], [], []
    for l in range(depth):
        x, wk, wv, st = _mixer(x, rope_p, mk_b, mv_b, None, lws, l, tile=mix_tile, chunk=CHUNK, alpha=alpha)
        x = _mlp(x.reshape(bp * sp, D_MODEL), lws, l, tile=mlp_tile, alpha=alpha).reshape(bp, sp, D_MODEL)
        wk_p.append(wk)
        wv_p.append(wv)
        gs_p.append(st)
    y_prompt = x

    bs, ts, _ = x_sample.shape
    keep = cache_win_k.shape[2]
    rope_s = _rope_tables(PAST_LEN + jnp.arange(ts, dtype=jnp.int32))
    cache = (cache_win_k.reshape(depth, bs, keep, A_KV), cache_win_v.reshape(depth, bs, keep, A_KV), state_gla)
    mk_s = cache_mem_k.reshape(depth, bs, n_mem, M_Q).astype(bf16)
    mv_s = cache_mem_v.reshape(depth, bs, n_mem, M_Q).astype(bf16)
    x = x_sample
    wk_s, wv_s, gs_s = [], [], []
    for l in range(depth):
        x, wk, wv, st = _mixer(x, rope_s, mk_s, mv_s, cache, lws, l, tile=ts, chunk=ts, alpha=alpha)
        x = _mlp(x.reshape(bs * ts, D_MODEL), lws, l, tile=_pick_tile(bs * ts, 512), alpha=alpha)
        x = x.reshape(bs, ts, D_MODEL)
        wk_s.append(wk)
        wv_s.append(wv)
        gs_s.append(st)
    y_sample = x

    kv_shape = lambda b: (depth, b, keep, A_KV_HEADS, A_HEAD_DIM)
    mem_shape = (depth, bp, n_mem, M_HEADS, M_HEAD_DIM)
    return (y_prompt, y_sample,
            jnp.stack(wk_p).reshape(kv_shape(bp)), jnp.stack(wv_p).reshape(kv_shape(bp)), jnp.stack(gs_p),
            mk_p.reshape(mem_shape), mv_p.reshape(mem_shape),
            jnp.stack(wk_s).reshape(kv_shape(bs)), jnp.stack(wv_s).reshape(kv_shape(bs)), jnp.stack(gs_s))
```

```python
import functools

import jax
import jax.numpy as jnp
from jax import lax
from jax.experimental import pallas as pl
from jax.experimental.pallas import tpu as pltpu

D_MODEL = 1024
PAST_LEN = 1024
CHUNK = 64
WINDOW = 128
A_HEADS = 8
A_KV_HEADS = 2
A_HEAD_DIM = 64
A_GROUP = A_HEADS // A_KV_HEADS
A_SCALE = A_HEAD_DIM ** -0.5
ROT_DIM = A_HEAD_DIM // 4
ROPE_THETA = 500000.0
A_Q = A_HEADS * A_HEAD_DIM
A_KV = A_KV_HEADS * A_HEAD_DIM
B_HEADS = 4
B_DK = 128
B_DV = 256
B_QK = B_HEADS * B_DK
B_V = B_HEADS * B_DV
GATE_RANK = 16
GATE_TAU = 16.0
M_HEADS = 4
M_HEAD_DIM = 128
M_Q = M_HEADS * M_HEAD_DIM
D_FF = 4 * D_MODEL
N_BRANCH = 3
LN_EPS = 1e-5
RMS_EPS = 1e-6
NEG_INF = -1e30

LANES = 128
GK_PAD = LANES
VMEM_LIMIT_BYTES = 60000 * 1024
GATE_BLOCK = 512
MLP_SUB = 256
GLA_SUB = 256
MIX_TILE = 512
OUT_ROWS = 256
MIXER_ORDER = ("A M B A ga B A M B A ga A A M A B M A gm A M A B A M A gm A M A B A gb A B A M A B A gb"
               " B A B B")

_C_AQ = 0
_C_AK = _C_AQ + A_Q
_C_AV = _C_AK + A_KV
_C_GK = _C_AV + A_KV
_C_BQ = _C_GK + GK_PAD
_C_BK = _C_BQ + B_QK
_C_BV = _C_BK + B_QK
_C_BG = _C_BV + B_V
_C_MQ = _C_BG + B_V
_C_GT = _C_MQ + M_Q
_C_END = _C_GT + N_BRANCH * D_MODEL

_NT = (((1,), (1,)), ((), ()))
_TN = (((0,), (0,)), ((), ()))

bf16 = jnp.bfloat16
f32 = jnp.float32


def _dot(a, b):
    return jnp.dot(a, b, preferred_element_type=f32)


def _dot_nt(a, b):
    return lax.dot_general(a, b, _NT, preferred_element_type=f32)


def _dot_tn(a, b):
    return lax.dot_general(a, b, _TN, preferred_element_type=f32)


def _layer_norm(z, g, b):
    mu = jnp.mean(z, axis=-1, keepdims=True)
    zc = z - mu
    var = jnp.mean(zc * zc, axis=-1, keepdims=True)
    return zc * lax.rsqrt(var + LN_EPS) * g + b


def _sigmoid(z):
    return 0.5 * jnp.tanh(0.5 * z) + 0.5


def _mixer_kernel(*refs, layer, tile, chunk, keep, has_cache, mask_start, alpha):
    it = iter(refs)
    sinks_ref = next(it)
    x_ref = next(it)
    cos_ref, sin_lo_ref, sin_hi_ref = next(it), next(it), next(it)
    mk_ref, mv_ref = next(it), next(it)
    if has_cache:
        ck_ref, cv_ref, s0_ref = next(it), next(it), next(it)
    w_in_ref, w_gk2_ref, b_gk_ref, gn_ref = next(it), next(it), next(it), next(it)
    wpa_ref, wpb_ref, wpm_ref, wout_ref = next(it), next(it), next(it), next(it)
    ln_g_ref, ln_b_ref = next(it), next(it)
    y_ref, wk_ref, wv_ref, st_ref = next(it), next(it), next(it), next(it)
    klo_s, khi_s, v_s, st_s = next(it), next(it), next(it), next(it)

    t = pl.program_id(1)
    last_t = pl.num_programs(1) - 1
    n_ch = tile // chunk
    win = keep + chunk

    lane = lax.broadcasted_iota(jnp.int32, (1, LANES), 1)
    low_half = lane < A_HEAD_DIM

    @pl.when(t == 0)
    def _init():
        if has_cache:
            ck = ck_ref[0]
            klo_s[0:keep, :] = jnp.where(low_half, ck, 0.0).astype(bf16)
            khi_s[0:keep, :] = jnp.where(low_half, 0.0, ck).astype(bf16)
            v_s[0:keep, :] = cv_ref[0].astype(bf16)
            for h in range(B_HEADS):
                st_s[h] = s0_ref[0, h].T
        else:
            zero = jnp.zeros((keep, LANES), bf16)
            klo_s[0:keep, :] = zero
            khi_s[0:keep, :] = zero
            v_s[0:keep, :] = zero
            st_s[...] = jnp.zeros(st_s.shape, f32)

    x = x_ref[0]
    xb = x.astype(bf16)
    res = {}

    def gated(key, proj):
        if "gate_" + key in res:
            res["gated_" + key] = res.pop("gate_" + key) * proj
        else:
            res["proj_" + key] = proj

    def gate_task(key, col):
        blocks = []
        n_blocks = D_MODEL // GATE_BLOCK
        for j in range(n_blocks):
            cs = slice(col + j * GATE_BLOCK, col + (j + 1) * GATE_BLOCK)
            blocks.append(_sigmoid(_dot(xb, w_in_ref[:, cs])))
            if j == n_blocks - 1:
                gate = jnp.concatenate(blocks, axis=1) if n_blocks > 1 else blocks[0]
                if "proj_" + key in res:
                    res["gated_" + key] = gate * res.pop("proj_" + key)
                else:
                    res["gate_" + key] = gate
            yield

    def task_a():
        cos = cos_ref[...]
        sin_lo = sin_lo_ref[...]
        sin_hi = sin_hi_ref[...]

        def rope(z):
            return (z * cos + pltpu.roll(z, LANES - ROT_DIM // 2, 1) * sin_lo
                    + pltpu.roll(z, ROT_DIM // 2, 1) * sin_hi)

        qkv = _dot(xb, w_in_ref[:, _C_AQ:_C_BQ])
        res["gk1"] = qkv[:, _C_GK:_C_BQ].astype(bf16)
        yield
        q_cols = [(rope(qkv[:, j * LANES:(j + 1) * LANES]) * A_SCALE).astype(bf16) for j in range(A_GROUP)]
        k_r = rope(qkv[:, _C_AK:_C_AV])
        v_new = qkv[:, _C_AV:_C_GK]
        klo_s[keep:keep + tile, :] = jnp.where(low_half, k_r, 0.0).astype(bf16)
        khi_s[keep:keep + tile, :] = jnp.where(low_half, 0.0, k_r).astype(bf16)
        v_s[keep:keep + tile, :] = v_new.astype(bf16)

        def emit(ref, lo, hi, rows):
            for g in range(A_KV_HEADS):
                ref[0, lo:hi, g, :] = rows[:, g * A_HEAD_DIM:(g + 1) * A_HEAD_DIM]

        if tile >= keep:
            emit(wk_ref, 0, keep, k_r[tile - keep:, :])
            emit(wv_ref, 0, keep, v_new[tile - keep:, :])
        else:
            emit(wk_ref, 0, keep - tile, ck_ref[0, tile:keep, :])
            emit(wv_ref, 0, keep - tile, cv_ref[0, tile:keep, :])
            emit(wk_ref, keep - tile, keep, k_r)
            emit(wv_ref, keep - tile, keep, v_new)
        yield

        rows = lax.broadcasted_iota(jnp.int32, (A_GROUP * chunk, 1), 0)
        key_col = lax.broadcasted_iota(jnp.int32, (1, win), 1)
        sinks = []
        for g in range(A_KV_HEADS):
            sk = jnp.full((A_GROUP * chunk, 1), sinks_ref[layer, g * A_GROUP + A_GROUP - 1], f32)
            for j in range(A_GROUP - 2, -1, -1):
                sk = jnp.where(rows < (j + 1) * chunk, sinks_ref[layer, g * A_GROUP + j], sk)
            sinks.append(sk)

        def scores(c):
            q_st = jnp.concatenate([qc[c * chunk:(c + 1) * chunk, :] for qc in q_cols], axis=0)
            out = []
            for k_ref in (klo_s, khi_s):
                s = _dot_nt(q_st, k_ref[c * chunk:c * chunk + win, :])
                if mask_start:
                    first_valid = keep - (t * tile + c * chunk)
                    s = jnp.where(key_col >= first_valid, s, NEG_INF)
                out.append(s)
            return out

        def softmax(s, sk):
            m = jnp.maximum(jnp.max(s, axis=-1, keepdims=True), sk)
            p = jnp.exp(s - m)
            den = jnp.sum(p, axis=-1, keepdims=True) + jnp.exp(sk - m)
            return (p * (1.0 / den)).astype(bf16)

        def values(c, probs):
            o_groups = [_dot(p, v_s[c * chunk:c * chunk + win, :]) for p in probs]
            o_sel = jnp.where(low_half, o_groups[0], o_groups[1])
            return jnp.concatenate([o_sel[j * chunk:(j + 1) * chunk, :] for j in range(A_GROUP)], axis=1)

        oa_chunks = []
        s_next = scores(0)
        yield
        for c in range(n_ch):
            s_cur = s_next
            if c + 1 < n_ch:
                s_next = scores(c + 1)
            probs = [softmax(s, sk) for s, sk in zip(s_cur, sinks)]
            yield
            oa_chunks.append(values(c, probs))
            yield
        o_a = jnp.concatenate(oa_chunks, axis=0).astype(bf16) if n_ch > 1 else oa_chunks[0].astype(bf16)

        if tile >= keep:
            klo_s[0:keep, :] = klo_s[tile:tile + keep, :]
            khi_s[0:keep, :] = khi_s[tile:tile + keep, :]
            v_s[0:keep, :] = v_s[tile:tile + keep, :]
        gated("a", _dot(o_a, wpa_ref[...]))

    def task_m():
        mq = _dot(xb, w_in_ref[:, _C_MQ:_C_GT]).astype(bf16)
        yield
        heads = [slice(h * M_HEAD_DIM, (h + 1) * M_HEAD_DIM) for h in range(M_HEADS)]

        def scores(hs):
            return _dot_nt(mq[:, hs], mk_ref[0, :, hs]) * (M_HEAD_DIM ** -0.5)

        def softmax(s):
            m = jnp.max(s, axis=-1, keepdims=True)
            p = jnp.exp(s - m)
            return (p * (1.0 / jnp.sum(p, axis=-1, keepdims=True))).astype(bf16)

        om_heads = []
        for i in range(0, M_HEADS, 2):
            pair = heads[i:i + 2]
            s_pair = [scores(hs) for hs in pair]
            yield
            p_pair = [softmax(s) for s in s_pair]
            yield
            om_heads += [_dot(p, mv_ref[0, :, hs]) for p, hs in zip(p_pair, pair)]
            yield
        o_m = jnp.concatenate(om_heads, axis=1).astype(bf16)
        gated("m", _dot(o_m, wpm_ref[...]))

    def task_b():
        gk = _dot(res["gk1"], w_gk2_ref[...]) + b_gk_ref[...]
        log_a = (jnp.minimum(gk, 0.0) - jnp.log(1.0 + jnp.exp(-jnp.abs(gk)))) * (1.0 / GATE_TAU)
        yield
        gsub = min(tile, GLA_SUB)
        subs = [slice(s * gsub, (s + 1) * gsub) for s in range(tile // gsub)]
        r_i = lax.broadcasted_iota(jnp.int32, (gsub, gsub), 0)
        c_i = lax.broadcasted_iota(jnp.int32, (gsub, gsub), 1)
        if gsub > chunk:
            causal = (c_i <= r_i) & ((r_i // chunk) == (c_i // chunk))
        else:
            causal = c_i <= r_i
        tri = jnp.where(causal, 1.0, 0.0).astype(bf16)
        la_hi = log_a.astype(bf16)
        la_lo = (log_a - la_hi.astype(f32)).astype(bf16)
        cum = [_dot(tri, la_hi[rs, :]) + _dot(tri, la_lo[rs, :]) for rs in subs]
        cum = jnp.concatenate(cum, axis=0) if len(subs) > 1 else cum[0]
        last_rows = [cum[(c + 1) * chunk - 1:(c + 1) * chunk, :] for c in range(n_ch)]
        if n_ch > 1:
            last = jnp.concatenate([jnp.broadcast_to(r, (chunk, B_QK)) for r in last_rows], axis=0)
        else:
            last = last_rows[0]
        decay = [jnp.exp(r) for r in last_rows]
        yield
        bq = _dot(xb, w_in_ref[:, _C_BQ:_C_BK])
        q_dec = (bq * (B_DK ** -0.5) * jnp.exp(cum)).astype(bf16)
        yield
        bk = _dot(xb, w_in_ref[:, _C_BK:_C_BV])
        k_inv = (bk * jnp.exp(-cum)).astype(bf16)
        k_end = (bk * jnp.exp(last - cum)).astype(bf16)
        yield
        gn = gn_ref[...]
        ob_heads = []
        for h0 in range(0, B_HEADS, 2):
            pair = (h0, h0 + 1)
            ks = [slice(h * B_DK, (h + 1) * B_DK) for h in pair]
            vv = _dot(xb, w_in_ref[:, _C_BV + h0 * B_DV:_C_BV + (h0 + 2) * B_DV]).astype(bf16)
            vh = [vv[:, :B_DV], vv[:, B_DV:]]
            qd = [q_dec[:, s] for s in ks]
            ke = [k_end[:, s] for s in ks]
            att = [[jnp.where(causal, _dot_nt(qd[i][rs, :], k_inv[rs, ks[i]]), 0.0).astype(bf16) for i in range(2)]
                   for rs in subs]
            yield
            o_intra = [[_dot(att[s][i], vh[i][rs, :]) for i in range(2)] for s, rs in enumerate(subs)]
            o_intra = [jnp.concatenate([o[i] for o in o_intra], axis=0) if len(subs) > 1 else o_intra[0][i]
                       for i in range(2)]
            chunks = [slice(c * chunk, (c + 1) * chunk) for c in range(n_ch)]
            upd = [[_dot_tn(vh[i][cs, :], ke[i][cs, :]) for i in range(2)] for cs in chunks]
            state = [st_s[h] for h in pair]
            o_inter = [[], []]
            for c, cs in enumerate(chunks):
                for i in range(2):
                    o_inter[i].append(_dot_nt(qd[i][cs, :], state[i].astype(bf16)))
                state = [state[i] * decay[c][:, ks[i]] + upd[c][i] for i in range(2)]
            for i, h in enumerate(pair):
                st_s[h] = state[i]
            yield
            gg = _dot(xb, w_in_ref[:, _C_BG + h0 * B_DV:_C_BG + (h0 + 2) * B_DV])
            for i in range(2):
                o = o_intra[i] + (jnp.concatenate(o_inter[i], axis=0) if n_ch > 1 else o_inter[i][0])
                o = o * lax.rsqrt(jnp.mean(o * o, axis=-1, keepdims=True) + RMS_EPS) * gn
                g_h = gg[:, i * B_DV:(i + 1) * B_DV]
                ob_heads.append((o * (g_h * _sigmoid(g_h))).astype(bf16))
            yield
        o_b = jnp.concatenate(ob_heads, axis=1)
        gated("b", _dot(o_b, wpb_ref[...]))

    tasks = {
        "A": task_a(), "M": task_m(), "B": task_b(),
        "ga": gate_task("a", _C_GT), "gb": gate_task("b", _C_GT + D_MODEL),
        "gm": gate_task("m", _C_GT + 2 * D_MODEL),
    }
    for name in MIXER_ORDER.split():
        next(tasks[name], None)
    for gen in tasks.values():
        for _ in gen:
            pass

    for key in "abm":
        if "gated_" + key not in res:
            res["gated_" + key] = res["gate_" + key] * res["proj_" + key]
    rb = min(tile, OUT_ROWS)
    pending = None
    for r in range(tile // rb):
        rs = slice(r * rb, (r + 1) * rb)
        merged = res["gated_a"][rs] + res["gated_b"][rs] + res["gated_m"][rs]
        z = alpha * x[rs] + _dot(merged.astype(bf16), wout_ref[...])
        if pending is not None:
            y_ref[0, pending[0], :] = _layer_norm(pending[1], ln_g_ref[...], ln_b_ref[...])
        pending = (rs, z)
    y_ref[0, pending[0], :] = _layer_norm(pending[1], ln_g_ref[...], ln_b_ref[...])

    @pl.when(t == last_t)
    def _emit_state():
        for h in range(B_HEADS):
            st_ref[0, h] = st_s[h].T


def _layer_spec(stacked, layer):
    rest = stacked.shape[1:]
    index = lambda *_: (layer,) + (0,) * len(rest)
    return pl.BlockSpec((None,) + rest, index, pipeline_mode=pl.Buffered(1))


def _mixer(x, rope_tabs, mk, mv, cache, lw, layer, *, tile, chunk, alpha):
    nb, seq, _ = x.shape
    n_mem = mk.shape[2]
    keep = WINDOW
    has_cache = cache is not None
    grid = (nb, seq // tile)
    assert seq % tile == 0 and tile % chunk == 0
    assert tile >= keep or (has_cache and grid[1] == 1)

    per_b = lambda b, t: (b, 0, 0)
    layer_b = lambda b, t: (layer, b, 0, 0)
    in_specs = [
        pl.BlockSpec(memory_space=pltpu.SMEM),
        pl.BlockSpec((1, tile, D_MODEL), lambda b, t: (b, t, 0)),
        pl.BlockSpec((tile, LANES), lambda b, t: (t, 0)),
        pl.BlockSpec((tile, LANES), lambda b, t: (t, 0)),
        pl.BlockSpec((tile, LANES), lambda b, t: (t, 0)),
        pl.BlockSpec((None, 1, n_mem, M_Q), layer_b),
        pl.BlockSpec((None, 1, n_mem, M_Q), layer_b),
    ]
    args = [lw["sinks"], x, *rope_tabs, mk, mv]
    if has_cache:
        in_specs += [
            pl.BlockSpec((None, 1, keep, LANES), layer_b),
            pl.BlockSpec((None, 1, keep, LANES), layer_b),
            pl.BlockSpec((None, 1, B_HEADS, B_DK, B_DV), lambda b, t: (layer, b, 0, 0, 0)),
        ]
        args += list(cache)
    weights = [lw["w_in"], lw["w_gk2"], lw["b_gk"], lw["gn"], lw["wpa"], lw["wpb"], lw["wpm"], lw["wout"],
               lw["ln1_g"], lw["ln1_b"]]
    in_specs += [_layer_spec(w, layer) for w in weights]
    args += weights

    out_shape = (
        jax.ShapeDtypeStruct((nb, seq, D_MODEL), f32),
        jax.ShapeDtypeStruct((nb, keep, A_KV_HEADS, A_HEAD_DIM), f32),
        jax.ShapeDtypeStruct((nb, keep, A_KV_HEADS, A_HEAD_DIM), f32),
        jax.ShapeDtypeStruct((nb, B_HEADS, B_DK, B_DV), f32),
    )
    out_specs = (
        pl.BlockSpec((1, tile, D_MODEL), lambda b, t: (b, t, 0)),
        pl.BlockSpec((1, keep, A_KV_HEADS, A_HEAD_DIM), lambda b, t: (b, 0, 0, 0)),
        pl.BlockSpec((1, keep, A_KV_HEADS, A_HEAD_DIM), lambda b, t: (b, 0, 0, 0)),
        pl.BlockSpec((1, B_HEADS, B_DK, B_DV), lambda b, t: (b, 0, 0, 0)),
    )
    scratch = [
        pltpu.VMEM((keep + tile, LANES), bf16),
        pltpu.VMEM((keep + tile, LANES), bf16),
        pltpu.VMEM((keep + tile, LANES), bf16),
        pltpu.VMEM((B_HEADS, B_DV, B_DK), f32),
    ]
    kern = functools.partial(_mixer_kernel, layer=layer, tile=tile, chunk=chunk, keep=keep, has_cache=has_cache,
                             mask_start=not has_cache, alpha=alpha)
    return pl.pallas_call(
        kern,
        grid=grid,
        in_specs=in_specs,
        out_specs=out_specs,
        out_shape=out_shape,
        scratch_shapes=scratch,
        compiler_params=pltpu.CompilerParams(
            dimension_semantics=("arbitrary", "arbitrary"), vmem_limit_bytes=VMEM_LIMIT_BYTES),
        name="mixer_sample" if has_cache else "mixer_prompt",
    )(*args)


def _mlp_kernel(x_ref, wup_ref, bup_ref, wdn_ref, bdn_ref, g_ref, b_ref, y_ref, *, ff_block, sub, alpha):
    pending = None

    def finish(rows, z):
        y_ref[rows, :] = _layer_norm(z, g_ref[...], b_ref[...])

    for r in range(x_ref.shape[0] // sub):
        rows = slice(r * sub, (r + 1) * sub)
        x = x_ref[rows, :]
        xb = x.astype(bf16)
        acc = None
        for j in range(D_FF // ff_block):
            fs = slice(j * ff_block, (j + 1) * ff_block)
            h = jnp.maximum(_dot(xb, wup_ref[:, fs]) + bup_ref[:, fs], 0.0)
            part = _dot((h * h).astype(bf16), wdn_ref[fs, :])
            acc = part if acc is None else acc + part
            if j == 0 and pending is not None:
                finish(*pending)
        pending = (rows, alpha * x + (acc + bdn_ref[...]))
    finish(*pending)


def _mlp(x2d, lw, layer, *, tile, alpha):
    n = x2d.shape[0]
    assert n % tile == 0
    weights = [lw["wup"], lw["bup"], lw["wdn"], lw["bdn"], lw["ln2_g"], lw["ln2_b"]]
    return pl.pallas_call(
        functools.partial(_mlp_kernel, ff_block=1024, sub=min(tile, MLP_SUB), alpha=alpha),
        grid=(n // tile,),
        in_specs=[pl.BlockSpec((tile, D_MODEL), lambda i: (i, 0))] + [_layer_spec(w, layer) for w in weights],
        out_specs=pl.BlockSpec((tile, D_MODEL), lambda i: (i, 0)),
        out_shape=jax.ShapeDtypeStruct((n, D_MODEL), f32),
        compiler_params=pltpu.CompilerParams(
            dimension_semantics=("arbitrary",), vmem_limit_bytes=VMEM_LIMIT_BYTES),
        name="mlp",
    )(x2d, *weights)


def _memkv_kernel(mem_ref, w_ref, k_ref, v_ref, kb_ref, vb_ref):
    rows, n_mem, _ = mem_ref.shape
    kv = _dot(mem_ref[...].reshape(rows * n_mem, D_MODEL).astype(bf16), w_ref[...])
    k = kv[:, :M_Q].reshape(rows, n_mem, M_Q)
    v = kv[:, M_Q:].reshape(rows, n_mem, M_Q)
    for h in range(M_HEADS):
        hs = slice(h * M_HEAD_DIM, (h + 1) * M_HEAD_DIM)
        k_ref[:, :, h, :] = k[:, :, hs]
        v_ref[:, :, h, :] = v[:, :, hs]
    kb_ref[...] = k.astype(bf16)
    vb_ref[...] = v.astype(bf16)


def _memkv(mem, w_mem_kv_b):
    depth = w_mem_kv_b.shape[0]
    nb, n_mem, _ = mem.shape
    rows = _pick_tile(nb, 4)
    out_spec = pl.BlockSpec((None, rows, n_mem, M_Q), lambda l, b: (l, b, 0, 0))
    return pl.pallas_call(
        _memkv_kernel,
        grid=(depth, nb // rows),
        in_specs=[pl.BlockSpec((rows, n_mem, D_MODEL), lambda l, b: (b, 0, 0)),
                  pl.BlockSpec((None, D_MODEL, 2 * M_Q), lambda l, b: (l, 0, 0))],
        out_specs=(pl.BlockSpec((None, rows, n_mem, M_HEADS, M_HEAD_DIM), lambda l, b: (l, b, 0, 0, 0)),) * 2
        + (out_spec,) * 2,
        out_shape=(jax.ShapeDtypeStruct((depth, nb, n_mem, M_HEADS, M_HEAD_DIM), f32),) * 2
        + (jax.ShapeDtypeStruct((depth, nb, n_mem, M_Q), bf16),) * 2,
        compiler_params=pltpu.CompilerParams(
            dimension_semantics=("arbitrary", "arbitrary"), vmem_limit_bytes=VMEM_LIMIT_BYTES),
        name="memkv",
    )(mem, w_mem_kv_b)


def _rope_tables(pos):
    half = ROT_DIM // 2
    inv = ROPE_THETA ** (-jnp.arange(half, dtype=f32) / half)
    ang = pos.astype(f32)[:, None] * inv[None, :]
    cos, sin = jnp.cos(ang), jnp.sin(ang)
    n = pos.shape[0]
    pad = jnp.zeros((n, A_HEAD_DIM - ROT_DIM), f32)
    zero = jnp.zeros((n, half), f32)
    cos_h = jnp.concatenate([cos, cos, jnp.ones_like(pad)], axis=1)
    lo_h = jnp.concatenate([-sin, zero, pad], axis=1)
    hi_h = jnp.concatenate([zero, sin, pad], axis=1)
    return tuple(jnp.concatenate([t_, t_], axis=1) for t_ in (cos_h, lo_h, hi_h))


_HEAD_ORDER = [h for j in range(A_GROUP) for h in (j, j + A_GROUP)]
_S_BQ = A_Q + 2 * A_KV
_S_GK = _S_BQ + 2 * B_QK + 2 * B_V
_S_MQ = _S_GK + GATE_RANK
_S_END = _S_MQ + M_Q + N_BRANCH * D_MODEL


def _relayout_kernel(src_ref, dst_ref):
    def move(dst_lo, src_lo, width):
        dst_ref[:, dst_lo:dst_lo + width] = src_ref[:, src_lo:src_lo + width].astype(bf16)

    for k, h in enumerate(_HEAD_ORDER):
        move(_C_AQ + k * A_HEAD_DIM, h * A_HEAD_DIM, A_HEAD_DIM)
    move(_C_AK, A_Q, 2 * A_KV)
    dst_ref[:, _C_GK:_C_BQ] = jnp.zeros((dst_ref.shape[0], GK_PAD), bf16)
    move(_C_GK, _S_GK, GATE_RANK)
    move(_C_BQ, _S_BQ, _S_GK - _S_BQ)
    move(_C_MQ, _S_MQ, _S_END - _S_MQ)


def _relayout_w_in(w_in):
    depth, d_model, d_in = w_in.shape
    assert d_in == _S_END
    rows = _pick_tile(d_model, 128)
    return pl.pallas_call(
        _relayout_kernel,
        grid=(depth, d_model // rows),
        in_specs=[pl.BlockSpec((None, rows, d_in), lambda l, r: (l, r, 0))],
        out_specs=pl.BlockSpec((None, rows, _C_END), lambda l, r: (l, r, 0)),
        out_shape=jax.ShapeDtypeStruct((depth, d_model, _C_END), bf16),
        compiler_params=pltpu.CompilerParams(
            dimension_semantics=("arbitrary", "arbitrary"), vmem_limit_bytes=VMEM_LIMIT_BYTES),
        name="relayout_w_in",
    )(w_in)


def _prep_weights(w_in, w_gk2, b_gk, attn_sinks, gla_norm_g, w_proj_a, w_proj_b, w_proj_m, w_out,
                  ln1_g, ln1_b, w_up, b_up, w_down, b_down, ln2_g, ln2_b):
    head_order = _HEAD_ORDER
    w_in_r = _relayout_w_in(w_in)
    wpa = jnp.concatenate([w_proj_a[:, h * A_HEAD_DIM:(h + 1) * A_HEAD_DIM, :] for h in head_order], axis=1)
    row = lambda v: v[:, None, :]
    return dict(
        sinks=attn_sinks,
        w_in=w_in_r,
        w_gk2=jnp.pad(w_gk2, ((0, 0), (0, GK_PAD - GATE_RANK), (0, 0))).astype(bf16),
        b_gk=row(b_gk), gn=row(gla_norm_g),
        wpa=wpa.astype(bf16), wpb=w_proj_b.astype(bf16), wpm=w_proj_m.astype(bf16), wout=w_out.astype(bf16),
        ln1_g=row(ln1_g), ln1_b=row(ln1_b),
        wup=w_up.astype(bf16), bup=row(b_up), wdn=w_down.astype(bf16), bdn=row(b_down),
        ln2_g=row(ln2_g), ln2_b=row(ln2_b),
    )


def _pick_tile(n, pref):
    tile = min(pref, n)
    while n % tile:
        tile //= 2
    return tile


def kernel(x_prompt, x_sample, cache_win_k, cache_win_v, state_gla, cache_mem_k, cache_mem_v, mem_prompt,
           w_in, w_gk2, b_gk, attn_sinks, gla_norm_g, w_mem_kv, w_proj_a, w_proj_b, w_proj_m, w_out,
           ln1_g, ln1_b, w_up, b_up, w_down, b_down, ln2_g, ln2_b):
    depth = w_in.shape[0]
    alpha = (2 * depth) ** 0.25
    lws = _prep_weights(w_in, w_gk2, b_gk, attn_sinks, gla_norm_g, w_proj_a, w_proj_b, w_proj_m, w_out,
                        ln1_g, ln1_b, w_up, b_up, w_down, b_down, ln2_g, ln2_b)

    bp, sp, _ = x_prompt.shape
    n_mem = mem_prompt.shape[1]
    mk_p, mv_p, mk_b, mv_b = _memkv(mem_prompt, w_mem_kv.astype(bf16))
    rope_p = _rope_tables(jnp.arange(sp, dtype=jnp.int32))
    mix_tile = _pick_tile(sp, MIX_TILE)
    mlp_tile = _pick_tile(bp * sp, 1024)
    x = x_prompt
    wk_p, wv_p, gs_p = [], [], []
    for l in range(depth):
        x, wk, wv, st = _mixer(x, rope_p, mk_b, mv_b, None, lws, l, tile=mix_tile, chunk=CHUNK, alpha=alpha)
        x = _mlp(x.reshape(bp * sp, D_MODEL), lws, l, tile=mlp_tile, alpha=alpha).reshape(bp, sp, D_MODEL)
        wk_p.append(wk)
        wv_p.append(wv)
        gs_p.append(st)
    y_prompt = x

    bs, ts, _ = x_sample.shape
    keep = cache_win_k.shape[2]
    rope_s = _rope_tables(PAST_LEN + jnp.arange(ts, dtype=jnp.int32))
    cache = (cache_win_k.reshape(depth, bs, keep, A_KV), cache_win_v.reshape(depth, bs, keep, A_KV), state_gla)
    mk_s = cache_mem_k.reshape(depth, bs, n_mem, M_Q).astype(bf16)
    mv_s = cache_mem_v.reshape(depth, bs, n_mem, M_Q).astype(bf16)
    x = x_sample
    wk_s, wv_s, gs_s = [], [], []
    for l in range(depth):
        x, wk, wv, st = _mixer(x, rope_s, mk_s, mv_s, cache, lws, l, tile=ts, chunk=ts, alpha=alpha)
        x = _mlp(x.reshape(bs * ts, D_MODEL), lws, l, tile=_pick_tile(bs * ts, 512), alpha=alpha)
        x = x.reshape(bs, ts, D_MODEL)
        wk_s.append(wk)
        wv_s.append(wv)
        gs_s.append(st)
    y_sample = x

    return (y_prompt, y_sample,
            jnp.stack(wk_p), jnp.stack(wv_p), jnp.stack(gs_p),
            mk_p, mv_p,
            jnp.stack(wk_s), jnp.stack(wv_s), jnp.stack(gs_s))
```

```python
import functools

import jax
import jax.numpy as jnp
from jax import lax
from jax.experimental import pallas as pl
from jax.experimental.pallas import tpu as pltpu

D_MODEL = 1024
PAST_LEN = 1024
CHUNK = 64
WINDOW = 128
A_HEADS = 8
A_KV_HEADS = 2
A_HEAD_DIM = 64
A_GROUP = A_HEADS // A_KV_HEADS
A_SCALE = A_HEAD_DIM ** -0.5
ROT_DIM = A_HEAD_DIM // 4
ROPE_THETA = 500000.0
A_Q = A_HEADS * A_HEAD_DIM
A_KV = A_KV_HEADS * A_HEAD_DIM
B_HEADS = 4
B_DK = 128
B_DV = 256
B_QK = B_HEADS * B_DK
B_V = B_HEADS * B_DV
GATE_RANK = 16
GATE_TAU = 16.0
M_HEADS = 4
M_HEAD_DIM = 128
M_Q = M_HEADS * M_HEAD_DIM
D_FF = 4 * D_MODEL
N_BRANCH = 3
LN_EPS = 1e-5
RMS_EPS = 1e-6
NEG_INF = -1e30

LANES = 128
GK_PAD = LANES
VMEM_LIMIT_BYTES = 60000 * 1024
GATE_BLOCK = 512
MLP_SUB = 256
GLA_SUB = 256
MIX_TILE = 512
OUT_ROWS = 256
MIXER_ORDER = ("A M B A ga B A M B A ga A A M A B M A gm A M A B A M A gm A M A B A gb A B A M A B A gb"
               " B A B B")

_C_AQ = 0
_C_AK = _C_AQ + A_Q
_C_AV = _C_AK + A_KV
_C_GK = _C_AV + A_KV
_C_BQ = _C_GK + GK_PAD
_C_BK = _C_BQ + B_QK
_C_BV = _C_BK + B_QK
_C_BG = _C_BV + B_V
_C_MQ = _C_BG + B_V
_C_GT = _C_MQ + M_Q
_C_END = _C_GT + N_BRANCH * D_MODEL

_NT = (((1,), (1,)), ((), ()))
_TN = (((0,), (0,)), ((), ()))

bf16 = jnp.bfloat16
f32 = jnp.float32


def _dot(a, b):
    return jnp.dot(a, b, preferred_element_type=f32)


def _dot_nt(a, b):
    return lax.dot_general(a, b, _NT, preferred_element_type=f32)


def _dot_tn(a, b):
    return lax.dot_general(a, b, _TN, preferred_element_type=f32)


def _layer_norm(z, g, b):
    mu = jnp.mean(z, axis=-1, keepdims=True)
    zc = z - mu
    var = jnp.mean(zc * zc, axis=-1, keepdims=True)
    return zc * lax.rsqrt(var + LN_EPS) * g + b


def _sigmoid(z):
    return 0.5 * jnp.tanh(0.5 * z) + 0.5


def _mixer_kernel(*refs, layer, tile, chunk, keep, has_cache, mask_start, alpha):
    it = iter(refs)
    sinks_ref = next(it)
    x_ref = next(it)
    cos_ref, sin_lo_ref, sin_hi_ref = next(it), next(it), next(it)
    mk_ref, mv_ref = next(it), next(it)
    if has_cache:
        ck_ref, cv_ref, s0_ref = next(it), next(it), next(it)
    w_in_ref, w_gk2_ref, b_gk_ref, gn_ref = next(it), next(it), next(it), next(it)
    wpa_ref, wpb_ref, wpm_ref, wout_ref = next(it), next(it), next(it), next(it)
    ln_g_ref, ln_b_ref = next(it), next(it)
    y_ref, wk_ref, wv_ref, st_ref = next(it), next(it), next(it), next(it)
    klo_s, khi_s, v_s, st_s = next(it), next(it), next(it), next(it)

    t = pl.program_id(1)
    last_t = pl.num_programs(1) - 1
    n_ch = tile // chunk
    win = keep + chunk

    lane = lax.broadcasted_iota(jnp.int32, (1, LANES), 1)
    low_half = lane < A_HEAD_DIM

    @pl.when(t == 0)
    def _init():
        if has_cache:
            ck = ck_ref[0]
            klo_s[0:keep, :] = jnp.where(low_half, ck, 0.0).astype(bf16)
            khi_s[0:keep, :] = jnp.where(low_half, 0.0, ck).astype(bf16)
            v_s[0:keep, :] = cv_ref[0].astype(bf16)
            for h in range(B_HEADS):
                st_s[h] = s0_ref[0, h]
        else:
            zero = jnp.zeros((keep, LANES), bf16)
            klo_s[0:keep, :] = zero
            khi_s[0:keep, :] = zero
            v_s[0:keep, :] = zero
            st_s[...] = jnp.zeros(st_s.shape, f32)

    x = x_ref[0]
    xb = x.astype(bf16)
    res = {}

    def gated(key, proj):
        if "gate_" + key in res:
            res["gated_" + key] = res.pop("gate_" + key) * proj
        else:
            res["proj_" + key] = proj

    def gate_task(key, col):
        blocks = []
        n_blocks = D_MODEL // GATE_BLOCK
        for j in range(n_blocks):
            cs = slice(col + j * GATE_BLOCK, col + (j + 1) * GATE_BLOCK)
            blocks.append(_sigmoid(_dot(xb, w_in_ref[:, cs])))
            if j == n_blocks - 1:
                gate = jnp.concatenate(blocks, axis=1) if n_blocks > 1 else blocks[0]
                if "proj_" + key in res:
                    res["gated_" + key] = gate * res.pop("proj_" + key)
                else:
                    res["gate_" + key] = gate
            yield

    def task_a():
        cos = cos_ref[...]
        sin_lo = sin_lo_ref[...]
        sin_hi = sin_hi_ref[...]

        def rope(z):
            return (z * cos + pltpu.roll(z, LANES - ROT_DIM // 2, 1) * sin_lo
                    + pltpu.roll(z, ROT_DIM // 2, 1) * sin_hi)

        qkv = _dot(xb, w_in_ref[:, _C_AQ:_C_BQ])
        res["gk1"] = qkv[:, _C_GK:_C_BQ].astype(bf16)
        yield
        q_cols = [(rope(qkv[:, j * LANES:(j + 1) * LANES]) * A_SCALE).astype(bf16) for j in range(A_GROUP)]
        k_r = rope(qkv[:, _C_AK:_C_AV])
        v_new = qkv[:, _C_AV:_C_GK]
        klo_s[keep:keep + tile, :] = jnp.where(low_half, k_r, 0.0).astype(bf16)
        khi_s[keep:keep + tile, :] = jnp.where(low_half, 0.0, k_r).astype(bf16)
        v_s[keep:keep + tile, :] = v_new.astype(bf16)

        def emit(ref, lo, hi, rows):
            for g in range(A_KV_HEADS):
                ref[0, lo:hi, g, :] = rows[:, g * A_HEAD_DIM:(g + 1) * A_HEAD_DIM]

        if tile >= keep:
            emit(wk_ref, 0, keep, k_r[tile - keep:, :])
            emit(wv_ref, 0, keep, v_new[tile - keep:, :])
        else:
            emit(wk_ref, 0, keep - tile, ck_ref[0, tile:keep, :])
            emit(wv_ref, 0, keep - tile, cv_ref[0, tile:keep, :])
            emit(wk_ref, keep - tile, keep, k_r)
            emit(wv_ref, keep - tile, keep, v_new)
        yield

        rows = lax.broadcasted_iota(jnp.int32, (A_GROUP * chunk, 1), 0)
        key_col = lax.broadcasted_iota(jnp.int32, (1, win), 1)
        sinks = []
        for g in range(A_KV_HEADS):
            sk = jnp.full((A_GROUP * chunk, 1), sinks_ref[layer, g * A_GROUP + A_GROUP - 1], f32)
            for j in range(A_GROUP - 2, -1, -1):
                sk = jnp.where(rows < (j + 1) * chunk, sinks_ref[layer, g * A_GROUP + j], sk)
            sinks.append(sk)

        def scores(c):
            q_st = jnp.concatenate([qc[c * chunk:(c + 1) * chunk, :] for qc in q_cols], axis=0)
            out = []
            for k_ref in (klo_s, khi_s):
                s = _dot_nt(q_st, k_ref[c * chunk:c * chunk + win, :])
                if mask_start:
                    first_valid = keep - (t * tile + c * chunk)
                    s = jnp.where(key_col >= first_valid, s, NEG_INF)
                out.append(s)
            return out

        def softmax(s, sk):
            m = jnp.maximum(jnp.max(s, axis=-1, keepdims=True), sk)
            p = jnp.exp(s - m)
            den = jnp.sum(p, axis=-1, keepdims=True) + jnp.exp(sk - m)
            return (p * (1.0 / den)).astype(bf16)

        def values(c, probs):
            o_groups = [_dot(p, v_s[c * chunk:c * chunk + win, :]) for p in probs]
            o_sel = jnp.where(low_half, o_groups[0], o_groups[1])
            return jnp.concatenate([o_sel[j * chunk:(j + 1) * chunk, :] for j in range(A_GROUP)], axis=1)

        oa_chunks = []
        s_next = scores(0)
        yield
        for c in range(n_ch):
            s_cur = s_next
            if c + 1 < n_ch:
                s_next = scores(c + 1)
            probs = [softmax(s, sk) for s, sk in zip(s_cur, sinks)]
            yield
            oa_chunks.append(values(c, probs))
            yield
        o_a = jnp.concatenate(oa_chunks, axis=0).astype(bf16) if n_ch > 1 else oa_chunks[0].astype(bf16)

        if tile >= keep:
            klo_s[0:keep, :] = klo_s[tile:tile + keep, :]
            khi_s[0:keep, :] = khi_s[tile:tile + keep, :]
            v_s[0:keep, :] = v_s[tile:tile + keep, :]
        gated("a", _dot(o_a, wpa_ref[...]))

    def task_m():
        mq = _dot(xb, w_in_ref[:, _C_MQ:_C_GT]).astype(bf16)
        yield
        heads = [slice(h * M_HEAD_DIM, (h + 1) * M_HEAD_DIM) for h in range(M_HEADS)]

        def scores(hs):
            return _dot_nt(mq[:, hs], mk_ref[0, :, hs]) * (M_HEAD_DIM ** -0.5)

        def softmax(s):
            m = jnp.max(s, axis=-1, keepdims=True)
            p = jnp.exp(s - m)
            return (p * (1.0 / jnp.sum(p, axis=-1, keepdims=True))).astype(bf16)

        om_heads = []
        for i in range(0, M_HEADS, 2):
            pair = heads[i:i + 2]
            s_pair = [scores(hs) for hs in pair]
            yield
            p_pair = [softmax(s) for s in s_pair]
            yield
            om_heads += [_dot(p, mv_ref[0, :, hs]) for p, hs in zip(p_pair, pair)]
            yield
        o_m = jnp.concatenate(om_heads, axis=1).astype(bf16)
        gated("m", _dot(o_m, wpm_ref[...]))

    def task_b():
        gk = _dot(res["gk1"], w_gk2_ref[...]) + b_gk_ref[...]
        log_a = (jnp.minimum(gk, 0.0) - jnp.log(1.0 + jnp.exp(-jnp.abs(gk)))) * (1.0 / GATE_TAU)
        yield
        gsub = min(tile, GLA_SUB)
        subs = [slice(s * gsub, (s + 1) * gsub) for s in range(tile // gsub)]
        r_i = lax.broadcasted_iota(jnp.int32, (gsub, gsub), 0)
        c_i = lax.broadcasted_iota(jnp.int32, (gsub, gsub), 1)
        if gsub > chunk:
            causal = (c_i <= r_i) & ((r_i // chunk) == (c_i // chunk))
        else:
            causal = c_i <= r_i
        tri = jnp.where(causal, 1.0, 0.0).astype(bf16)
        la_hi = log_a.astype(bf16)
        la_lo = (log_a - la_hi.astype(f32)).astype(bf16)
        cum = [_dot(tri, la_hi[rs, :]) + _dot(tri, la_lo[rs, :]) for rs in subs]
        cum = jnp.concatenate(cum, axis=0) if len(subs) > 1 else cum[0]
        last_rows = [cum[(c + 1) * chunk - 1:(c + 1) * chunk, :] for c in range(n_ch)]
        if n_ch > 1:
            last = jnp.concatenate([jnp.broadcast_to(r, (chunk, B_QK)) for r in last_rows], axis=0)
        else:
            last = last_rows[0]
        pad_rows = jnp.zeros((LANES - n_ch, B_QK), f32)
        decay_t = jnp.exp(jnp.concatenate(last_rows + [pad_rows], axis=0)).T
        yield
        bq = _dot(xb, w_in_ref[:, _C_BQ:_C_BK])
        q_dec = (bq * (B_DK ** -0.5) * jnp.exp(cum)).astype(bf16)
        yield
        bk = _dot(xb, w_in_ref[:, _C_BK:_C_BV])
        k_inv = (bk * jnp.exp(-cum)).astype(bf16)
        k_end = (bk * jnp.exp(last - cum)).astype(bf16)
        yield
        gn = gn_ref[...]
        ob_heads = []
        for h0 in range(0, B_HEADS, 2):
            pair = (h0, h0 + 1)
            ks = [slice(h * B_DK, (h + 1) * B_DK) for h in pair]
            vv = _dot(xb, w_in_ref[:, _C_BV + h0 * B_DV:_C_BV + (h0 + 2) * B_DV]).astype(bf16)
            vh = [vv[:, :B_DV], vv[:, B_DV:]]
            qd = [q_dec[:, s] for s in ks]
            ke = [k_end[:, s] for s in ks]
            att = [[jnp.where(causal, _dot_nt(qd[i][rs, :], k_inv[rs, ks[i]]), 0.0).astype(bf16) for i in range(2)]
                   for rs in subs]
            yield
            o_intra = [[_dot(att[s][i], vh[i][rs, :]) for i in range(2)] for s, rs in enumerate(subs)]
            o_intra = [jnp.concatenate([o[i] for o in o_intra], axis=0) if len(subs) > 1 else o_intra[0][i]
                       for i in range(2)]
            chunks = [slice(c * chunk, (c + 1) * chunk) for c in range(n_ch)]
            upd = [[_dot_tn(ke[i][cs, :], vh[i][cs, :]) for i in range(2)] for cs in chunks]
            state = [st_s[h] for h in pair]
            o_inter = [[], []]
            for c, cs in enumerate(chunks):
                for i in range(2):
                    o_inter[i].append(_dot(qd[i][cs, :], state[i].astype(bf16)))
                state = [state[i] * decay_t[ks[i], c:c + 1] + upd[c][i] for i in range(2)]
            for i, h in enumerate(pair):
                st_s[h] = state[i]
            yield
            gg = _dot(xb, w_in_ref[:, _C_BG + h0 * B_DV:_C_BG + (h0 + 2) * B_DV])
            for i in range(2):
                o = o_intra[i] + (jnp.concatenate(o_inter[i], axis=0) if n_ch > 1 else o_inter[i][0])
                o = o * lax.rsqrt(jnp.mean(o * o, axis=-1, keepdims=True) + RMS_EPS) * gn
                g_h = gg[:, i * B_DV:(i + 1) * B_DV]
                ob_heads.append((o * (g_h * _sigmoid(g_h))).astype(bf16))
            yield
        o_b = jnp.concatenate(ob_heads, axis=1)
        gated("b", _dot(o_b, wpb_ref[...]))

    tasks = {
        "A": task_a(), "M": task_m(), "B": task_b(),
        "ga": gate_task("a", _C_GT), "gb": gate_task("b", _C_GT + D_MODEL),
        "gm": gate_task("m", _C_GT + 2 * D_MODEL),
    }
    for name in MIXER_ORDER.split():
        next(tasks[name], None)
    for gen in tasks.values():
        for _ in gen:
            pass

    for key in "abm":
        if "gated_" + key not in res:
            res["gated_" + key] = res["gate_" + key] * res["proj_" + key]
    rb = min(tile, OUT_ROWS)
    pending = None
    for r in range(tile // rb):
        rs = slice(r * rb, (r + 1) * rb)
        merged = res["gated_a"][rs] + res["gated_b"][rs] + res["gated_m"][rs]
        z = alpha * x[rs] + _dot(merged.astype(bf16), wout_ref[...])
        if pending is not None:
            y_ref[0, pending[0], :] = _layer_norm(pending[1], ln_g_ref[...], ln_b_ref[...])
        pending = (rs, z)
    y_ref[0, pending[0], :] = _layer_norm(pending[1], ln_g_ref[...], ln_b_ref[...])

    @pl.when(t == last_t)
    def _emit_state():
        st_ref[0] = st_s[...]


def _layer_spec(stacked, layer):
    rest = stacked.shape[1:]
    index = lambda *_: (layer,) + (0,) * len(rest)
    return pl.BlockSpec((None,) + rest, index, pipeline_mode=pl.Buffered(1))


def _mixer(x, rope_tabs, mk, mv, cache, lw, layer, *, tile, chunk, alpha):
    nb, seq, _ = x.shape
    n_mem = mk.shape[2]
    keep = WINDOW
    has_cache = cache is not None
    grid = (nb, seq // tile)
    assert seq % tile == 0 and tile % chunk == 0
    assert tile >= keep or (has_cache and grid[1] == 1)

    per_b = lambda b, t: (b, 0, 0)
    layer_b = lambda b, t: (layer, b, 0, 0)
    in_specs = [
        pl.BlockSpec(memory_space=pltpu.SMEM),
        pl.BlockSpec((1, tile, D_MODEL), lambda b, t: (b, t, 0)),
        pl.BlockSpec((tile, LANES), lambda b, t: (t, 0)),
        pl.BlockSpec((tile, LANES), lambda b, t: (t, 0)),
        pl.BlockSpec((tile, LANES), lambda b, t: (t, 0)),
        pl.BlockSpec((None, 1, n_mem, M_Q), layer_b),
        pl.BlockSpec((None, 1, n_mem, M_Q), layer_b),
    ]
    args = [lw["sinks"], x, *rope_tabs, mk, mv]
    if has_cache:
        in_specs += [
            pl.BlockSpec((None, 1, keep, LANES), layer_b),
            pl.BlockSpec((None, 1, keep, LANES), layer_b),
            pl.BlockSpec((None, 1, B_HEADS, B_DK, B_DV), lambda b, t: (layer, b, 0, 0, 0)),
        ]
        args += list(cache)
    weights = [lw["w_in"], lw["w_gk2"], lw["b_gk"], lw["gn"], lw["wpa"], lw["wpb"], lw["wpm"], lw["wout"],
               lw["ln1_g"], lw["ln1_b"]]
    in_specs += [_layer_spec(w, layer) for w in weights]
    args += weights

    out_shape = (
        jax.ShapeDtypeStruct((nb, seq, D_MODEL), f32),
        jax.ShapeDtypeStruct((nb, keep, A_KV_HEADS, A_HEAD_DIM), f32),
        jax.ShapeDtypeStruct((nb, keep, A_KV_HEADS, A_HEAD_DIM), f32),
        jax.ShapeDtypeStruct((nb, B_HEADS, B_DK, B_DV), f32),
    )
    out_specs = (
        pl.BlockSpec((1, tile, D_MODEL), lambda b, t: (b, t, 0)),
        pl.BlockSpec((1, keep, A_KV_HEADS, A_HEAD_DIM), lambda b, t: (b, 0, 0, 0)),
        pl.BlockSpec((1, keep, A_KV_HEADS, A_HEAD_DIM), lambda b, t: (b, 0, 0, 0)),
        pl.BlockSpec((1, B_HEADS, B_DK, B_DV), lambda b, t: (b, 0, 0, 0)),
    )
    scratch = [
        pltpu.VMEM((keep + tile, LANES), bf16),
        pltpu.VMEM((keep + tile, LANES), bf16),
        pltpu.VMEM((keep + tile, LANES), bf16),
        pltpu.VMEM((B_HEADS, B_DK, B_DV), f32),
    ]
    kern = functools.partial(_mixer_kernel, layer=layer, tile=tile, chunk=chunk, keep=keep, has_cache=has_cache,
                             mask_start=not has_cache, alpha=alpha)
    return pl.pallas_call(
        kern,
        grid=grid,
        in_specs=in_specs,
        out_specs=out_specs,
        out_shape=out_shape,
        scratch_shapes=scratch,
        compiler_params=pltpu.CompilerParams(
            dimension_semantics=("arbitrary", "arbitrary"), vmem_limit_bytes=VMEM_LIMIT_BYTES),
        name="mixer_sample" if has_cache else "mixer_prompt",
    )(*args)


def _mlp_kernel(x_ref, wup_ref, bup_ref, wdn_ref, bdn_ref, g_ref, b_ref, y_ref, *, ff_block, sub, alpha):
    pending = None

    def finish(rows, z):
        y_ref[rows, :] = _layer_norm(z, g_ref[...], b_ref[...])

    for r in range(x_ref.shape[0] // sub):
        rows = slice(r * sub, (r + 1) * sub)
        x = x_ref[rows, :]
        xb = x.astype(bf16)
        acc = None
        for j in range(D_FF // ff_block):
            fs = slice(j * ff_block, (j + 1) * ff_block)
            h = jnp.maximum(_dot(xb, wup_ref[:, fs]) + bup_ref[:, fs], 0.0)
            part = _dot((h * h).astype(bf16), wdn_ref[fs, :])
            acc = part if acc is None else acc + part
            if j == 0 and pending is not None:
                finish(*pending)
        pending = (rows, alpha * x + (acc + bdn_ref[...]))
    finish(*pending)


def _mlp(x2d, lw, layer, *, tile, alpha):
    n = x2d.shape[0]
    assert n % tile == 0
    weights = [lw["wup"], lw["bup"], lw["wdn"], lw["bdn"], lw["ln2_g"], lw["ln2_b"]]
    return pl.pallas_call(
        functools.partial(_mlp_kernel, ff_block=1024, sub=min(tile, MLP_SUB), alpha=alpha),
        grid=(n // tile,),
        in_specs=[pl.BlockSpec((tile, D_MODEL), lambda i: (i, 0))] + [_layer_spec(w, layer) for w in weights],
        out_specs=pl.BlockSpec((tile, D_MODEL), lambda i: (i, 0)),
        out_shape=jax.ShapeDtypeStruct((n, D_MODEL), f32),
        compiler_params=pltpu.CompilerParams(
            dimension_semantics=("arbitrary",), vmem_limit_bytes=VMEM_LIMIT_BYTES),
        name="mlp",
    )(x2d, *weights)


def _memkv_kernel(mem_ref, w_ref, k_ref, v_ref, kb_ref, vb_ref):
    rows, n_mem, _ = mem_ref.shape
    kv = _dot(mem_ref[...].reshape(rows * n_mem, D_MODEL).astype(bf16), w_ref[...])
    k = kv[:, :M_Q].reshape(rows, n_mem, M_Q)
    v = kv[:, M_Q:].reshape(rows, n_mem, M_Q)
    for h in range(M_HEADS):
        hs = slice(h * M_HEAD_DIM, (h + 1) * M_HEAD_DIM)
        k_ref[:, :, h, :] = k[:, :, hs]
        v_ref[:, :, h, :] = v[:, :, hs]
    kb_ref[...] = k.astype(bf16)
    vb_ref[...] = v.astype(bf16)


def _memkv(mem, w_mem_kv_b):
    depth = w_mem_kv_b.shape[0]
    nb, n_mem, _ = mem.shape
    rows = _pick_tile(nb, 4)
    out_spec = pl.BlockSpec((None, rows, n_mem, M_Q), lambda l, b: (l, b, 0, 0))
    return pl.pallas_call(
        _memkv_kernel,
        grid=(depth, nb // rows),
        in_specs=[pl.BlockSpec((rows, n_mem, D_MODEL), lambda l, b: (b, 0, 0)),
                  pl.BlockSpec((None, D_MODEL, 2 * M_Q), lambda l, b: (l, 0, 0))],
        out_specs=(pl.BlockSpec((None, rows, n_mem, M_HEADS, M_HEAD_DIM), lambda l, b: (l, b, 0, 0, 0)),) * 2
        + (out_spec,) * 2,
        out_shape=(jax.ShapeDtypeStruct((depth, nb, n_mem, M_HEADS, M_HEAD_DIM), f32),) * 2
        + (jax.ShapeDtypeStruct((depth, nb, n_mem, M_Q), bf16),) * 2,
        compiler_params=pltpu.CompilerParams(
            dimension_semantics=("arbitrary", "arbitrary"), vmem_limit_bytes=VMEM_LIMIT_BYTES),
        name="memkv",
    )(mem, w_mem_kv_b)


def _rope_tables(pos):
    half = ROT_DIM // 2
    inv = ROPE_THETA ** (-jnp.arange(half, dtype=f32) / half)
    ang = pos.astype(f32)[:, None] * inv[None, :]
    cos, sin = jnp.cos(ang), jnp.sin(ang)
    n = pos.shape[0]
    pad = jnp.zeros((n, A_HEAD_DIM - ROT_DIM), f32)
    zero = jnp.zeros((n, half), f32)
    cos_h = jnp.concatenate([cos, cos, jnp.ones_like(pad)], axis=1)
    lo_h = jnp.concatenate([-sin, zero, pad], axis=1)
    hi_h = jnp.concatenate([zero, sin, pad], axis=1)
    return tuple(jnp.concatenate([t_, t_], axis=1) for t_ in (cos_h, lo_h, hi_h))


_HEAD_ORDER = [h for j in range(A_GROUP) for h in (j, j + A_GROUP)]
_S_BQ = A_Q + 2 * A_KV
_S_GK = _S_BQ + 2 * B_QK + 2 * B_V
_S_MQ = _S_GK + GATE_RANK
_S_END = _S_MQ + M_Q + N_BRANCH * D_MODEL


def _relayout_kernel(src_ref, dst_ref):
    def move(dst_lo, src_lo, width):
        dst_ref[:, dst_lo:dst_lo + width] = src_ref[:, src_lo:src_lo + width].astype(bf16)

    for k, h in enumerate(_HEAD_ORDER):
        move(_C_AQ + k * A_HEAD_DIM, h * A_HEAD_DIM, A_HEAD_DIM)
    move(_C_AK, A_Q, 2 * A_KV)
    dst_ref[:, _C_GK:_C_BQ] = jnp.zeros((dst_ref.shape[0], GK_PAD), bf16)
    move(_C_GK, _S_GK, GATE_RANK)
    move(_C_BQ, _S_BQ, _S_GK - _S_BQ)
    move(_C_MQ, _S_MQ, _S_END - _S_MQ)


def _relayout_w_in(w_in):
    depth, d_model, d_in = w_in.shape
    assert d_in == _S_END
    rows = _pick_tile(d_model, 128)
    return pl.pallas_call(
        _relayout_kernel,
        grid=(depth, d_model // rows),
        in_specs=[pl.BlockSpec((None, rows, d_in), lambda l, r: (l, r, 0))],
        out_specs=pl.BlockSpec((None, rows, _C_END), lambda l, r: (l, r, 0)),
        out_shape=jax.ShapeDtypeStruct((depth, d_model, _C_END), bf16),
        compiler_params=pltpu.CompilerParams(
            dimension_semantics=("arbitrary", "arbitrary"), vmem_limit_bytes=VMEM_LIMIT_BYTES),
        name="relayout_w_in",
    )(w_in)


def _prep_weights(w_in, w_gk2, b_gk, attn_sinks, gla_norm_g, w_proj_a, w_proj_b, w_proj_m, w_out,
                  ln1_g, ln1_b, w_up, b_up, w_down, b_down, ln2_g, ln2_b):
    head_order = _HEAD_ORDER
    w_in_r = _relayout_w_in(w_in)
    wpa = jnp.concatenate([w_proj_a[:, h * A_HEAD_DIM:(h + 1) * A_HEAD_DIM, :] for h in head_order], axis=1)
    row = lambda v: v[:, None, :]
    return dict(
        sinks=attn_sinks,
        w_in=w_in_r,
        w_gk2=jnp.pad(w_gk2, ((0, 0), (0, GK_PAD - GATE_RANK), (0, 0))).astype(bf16),
        b_gk=row(b_gk), gn=row(gla_norm_g),
        wpa=wpa.astype(bf16), wpb=w_proj_b.astype(bf16), wpm=w_proj_m.astype(bf16), wout=w_out.astype(bf16),
        ln1_g=row(ln1_g), ln1_b=row(ln1_b),
        wup=w_up.astype(bf16), bup=row(b_up), wdn=w_down.astype(bf16), bdn=row(b_down),
        ln2_g=row(ln2_g), ln2_b=row(ln2_b),
    )


def _pick_tile(n, pref):
    tile = min(pref, n)
    while n % tile:
        tile //= 2
    return tile


def kernel(x_prompt, x_sample, cache_win_k, cache_win_v, state_gla, cache_mem_k, cache_mem_v, mem_prompt,
           w_in, w_gk2, b_gk, attn_sinks, gla_norm_g, w_mem_kv, w_proj_a, w_proj_b, w_proj_m, w_out,
           ln1_g, ln1_b, w_up, b_up, w_down, b_down, ln2_g, ln2_b):
    depth = w_in.shape[0]
    alpha = (2 * depth) ** 0.25
    lws = _prep_weights(w_in, w_gk2, b_gk, attn_sinks, gla_norm_g, w_proj_a, w_proj_b, w_proj_m, w_out,
                        ln1_g, ln1_b, w_up, b_up, w_down, b_down, ln2_g, ln2_b)

    bp, sp, _ = x_prompt.shape
    n_mem = mem_prompt.shape[1]
    mk_p, mv_p, mk_b, mv_b = _memkv(mem_prompt, w_mem_kv.astype(bf16))
    rope_p = _rope_tables(jnp.arange(sp, dtype=jnp.int32))
    mix_tile = _pick_tile(sp, MIX_TILE)
    mlp_tile = _pick_tile(bp * sp, 1024)
    x = x_prompt
    wk_p, wv_p, gs_p = [], [], []
    for l in range(depth):
        x, wk, wv, st = _mixer(x, rope_p, mk_b, mv_b, None, lws, l, tile=mix_tile, chunk=CHUNK, alpha=alpha)
        x = _mlp(x.reshape(bp * sp, D_MODEL), lws, l, tile=mlp_tile, alpha=alpha).reshape(bp, sp, D_MODEL)
        wk_p.append(wk)
        wv_p.append(wv)
        gs_p.append(st)
    y_prompt = x

    bs, ts, _ = x_sample.shape
    keep = cache_win_k.shape[2]
    rope_s = _rope_tables(PAST_LEN + jnp.arange(ts, dtype=jnp.int32))
    cache = (cache_win_k.reshape(depth, bs, keep, A_KV), cache_win_v.reshape(depth, bs, keep, A_KV), state_gla)
    mk_s = cache_mem_k.reshape(depth, bs, n_mem, M_Q).astype(bf16)
    mv_s = cache_mem_v.reshape(depth, bs, n_mem, M_Q).astype(bf16)
    x = x_sample
    wk_s, wv_s, gs_s = [], [], []
    for l in range(depth):
        x, wk, wv, st = _mixer(x, rope_s, mk_s, mv_s, cache, lws, l, tile=ts, chunk=ts, alpha=alpha)
        x = _mlp(x.reshape(bs * ts, D_MODEL), lws, l, tile=_pick_tile(bs * ts, 512), alpha=alpha)
        x = x.reshape(bs, ts, D_MODEL)
        wk_s.append(wk)
        wv_s.append(wv)
        gs_s.append(st)
    y_sample = x

    return (y_prompt, y_sample,
            jnp.stack(wk_p), jnp.stack(wv_p), jnp.stack(gs_p),
            mk_p, mv_p,
            jnp.stack(wk_s), jnp.stack(wv_s), jnp.stack(gs_s))
```

```python
import functools

import jax
import jax.numpy as jnp
from jax import lax
from jax.experimental import pallas as pl
from jax.experimental.pallas import tpu as pltpu

D_MODEL = 1024
PAST_LEN = 1024
CHUNK = 64
WINDOW = 128
A_HEADS = 8
A_KV_HEADS = 2
A_HEAD_DIM = 64
A_GROUP = A_HEADS // A_KV_HEADS
A_SCALE = A_HEAD_DIM ** -0.5
ROT_DIM = A_HEAD_DIM // 4
ROPE_THETA = 500000.0
A_Q = A_HEADS * A_HEAD_DIM
A_KV = A_KV_HEADS * A_HEAD_DIM
B_HEADS = 4
B_DK = 128
B_DV = 256
B_QK = B_HEADS * B_DK
B_V = B_HEADS * B_DV
GATE_RANK = 16
GATE_TAU = 16.0
M_HEADS = 4
M_HEAD_DIM = 128
M_Q = M_HEADS * M_HEAD_DIM
D_FF = 4 * D_MODEL
N_BRANCH = 3
LN_EPS = 1e-5
RMS_EPS = 1e-6
NEG_INF = -1e30

LANES = 128
GK_PAD = LANES
VMEM_LIMIT_BYTES = 60000 * 1024
GATE_BLOCK = 512
MLP_SUB = 256
GLA_SUB = 256
MIX_TILE = 512
OUT_ROWS = 256
MIXER_ORDER = ("A M B A ga B A M B A ga A A M A B M A gm A M A B A M A gm A M A B A gb A B A M A B A gb"
               " B A B B")

_C_AQ = 0
_C_AK = _C_AQ + A_Q
_C_AV = _C_AK + A_KV
_C_GK = _C_AV + A_KV
_C_BQ = _C_GK + GK_PAD
_C_BK = _C_BQ + B_QK
_C_BV = _C_BK + B_QK
_C_BG = _C_BV + B_V
_C_MQ = _C_BG + B_V
_C_GT = _C_MQ + M_Q
_C_END = _C_GT + N_BRANCH * D_MODEL

_NT = (((1,), (1,)), ((), ()))
_TN = (((0,), (0,)), ((), ()))

bf16 = jnp.bfloat16
f32 = jnp.float32


def _dot(a, b):
    return jnp.dot(a, b, preferred_element_type=f32)


def _dot_nt(a, b):
    return lax.dot_general(a, b, _NT, preferred_element_type=f32)


def _dot_tn(a, b):
    return lax.dot_general(a, b, _TN, preferred_element_type=f32)


def _layer_norm(z, g, b):
    mu = jnp.mean(z, axis=-1, keepdims=True)
    zc = z - mu
    var = jnp.mean(zc * zc, axis=-1, keepdims=True)
    return zc * lax.rsqrt(var + LN_EPS) * g + b


def _sigmoid(z):
    return 0.5 * jnp.tanh(0.5 * z) + 0.5


def _mixer_kernel(*refs, layer, tile, chunk, keep, has_cache, mask_start, alpha):
    it = iter(refs)
    sinks_ref = next(it)
    x_ref = next(it)
    cos_ref, sin_lo_ref, sin_hi_ref = next(it), next(it), next(it)
    mk_ref, mv_ref = next(it), next(it)
    if has_cache:
        ck_ref, cv_ref, s0_ref = next(it), next(it), next(it)
    w_in_ref, w_gk2_ref, b_gk_ref, gn_ref = next(it), next(it), next(it), next(it)
    wpa_ref, wpb_ref, wpm_ref, wout_ref = next(it), next(it), next(it), next(it)
    ln_g_ref, ln_b_ref = next(it), next(it)
    y_ref, wk_ref, wv_ref, st_ref = next(it), next(it), next(it), next(it)
    klo_s, khi_s, v_s, st_s = next(it), next(it), next(it), next(it)

    t = pl.program_id(1)
    last_t = pl.num_programs(1) - 1
    n_ch = tile // chunk
    win = keep + chunk

    lane = lax.broadcasted_iota(jnp.int32, (1, LANES), 1)
    low_half = lane < A_HEAD_DIM

    @pl.when(t == 0)
    def _init():
        if has_cache:
            ck = ck_ref[0]
            klo_s[0:keep, :] = jnp.where(low_half, ck, 0.0).astype(bf16)
            khi_s[0:keep, :] = jnp.where(low_half, 0.0, ck).astype(bf16)
            v_s[0:keep, :] = cv_ref[0].astype(bf16)
            for h in range(B_HEADS):
                st_s[h] = s0_ref[0, h]
        else:
            zero = jnp.zeros((keep, LANES), bf16)
            klo_s[0:keep, :] = zero
            khi_s[0:keep, :] = zero
            v_s[0:keep, :] = zero
            st_s[...] = jnp.zeros(st_s.shape, f32)

    x = x_ref[0]
    xb = x.astype(bf16)
    res = {}

    def gated(key, proj):
        if "gate_" + key in res:
            res["gated_" + key] = res.pop("gate_" + key) * proj
        else:
            res["proj_" + key] = proj

    def gate_task(key, col):
        blocks = []
        n_blocks = D_MODEL // GATE_BLOCK
        for j in range(n_blocks):
            cs = slice(col + j * GATE_BLOCK, col + (j + 1) * GATE_BLOCK)
            blocks.append(_sigmoid(_dot(xb, w_in_ref[:, cs])))
            if j == n_blocks - 1:
                gate = jnp.concatenate(blocks, axis=1) if n_blocks > 1 else blocks[0]
                if "proj_" + key in res:
                    res["gated_" + key] = gate * res.pop("proj_" + key)
                else:
                    res["gate_" + key] = gate
            yield

    def task_a():
        cos = cos_ref[...]
        sin_lo = sin_lo_ref[...]
        sin_hi = sin_hi_ref[...]

        def rope(z):
            return (z * cos + pltpu.roll(z, LANES - ROT_DIM // 2, 1) * sin_lo
                    + pltpu.roll(z, ROT_DIM // 2, 1) * sin_hi)

        qkv = _dot(xb, w_in_ref[:, _C_AQ:_C_BQ])
        res["gk1"] = qkv[:, _C_GK:_C_BQ].astype(bf16)
        yield
        q_cols = [(rope(qkv[:, j * LANES:(j + 1) * LANES]) * A_SCALE).astype(bf16) for j in range(A_GROUP)]
        k_r = rope(qkv[:, _C_AK:_C_AV])
        v_new = qkv[:, _C_AV:_C_GK]
        klo_s[keep:keep + tile, :] = jnp.where(low_half, k_r, 0.0).astype(bf16)
        khi_s[keep:keep + tile, :] = jnp.where(low_half, 0.0, k_r).astype(bf16)
        v_s[keep:keep + tile, :] = v_new.astype(bf16)

        res["k_tail"] = k_r[max(tile - keep, 0):, :]
        res["v_tail"] = v_new[max(tile - keep, 0):, :]
        yield

        rows = lax.broadcasted_iota(jnp.int32, (A_GROUP * chunk, 1), 0)
        key_col = lax.broadcasted_iota(jnp.int32, (1, win), 1)
        sinks = []
        for g in range(A_KV_HEADS):
            sk = jnp.full((A_GROUP * chunk, 1), sinks_ref[layer, g * A_GROUP + A_GROUP - 1], f32)
            for j in range(A_GROUP - 2, -1, -1):
                sk = jnp.where(rows < (j + 1) * chunk, sinks_ref[layer, g * A_GROUP + j], sk)
            sinks.append(sk)

        def scores(c):
            q_st = jnp.concatenate([qc[c * chunk:(c + 1) * chunk, :] for qc in q_cols], axis=0)
            out = []
            for k_ref in (klo_s, khi_s):
                s = _dot_nt(q_st, k_ref[c * chunk:c * chunk + win, :])
                if mask_start and c * chunk < keep:
                    first_valid = keep - (t * tile + c * chunk)
                    s = jnp.where(key_col >= first_valid, s, NEG_INF)
                out.append(s)
            return out

        def softmax(s, sk):
            m = jnp.maximum(jnp.max(s, axis=-1, keepdims=True), sk)
            p = jnp.exp(s - m)
            den = jnp.sum(p, axis=-1, keepdims=True) + jnp.exp(sk - m)
            return (p * (1.0 / den)).astype(bf16)

        def values(c, probs):
            o_groups = [_dot(p, v_s[c * chunk:c * chunk + win, :]) for p in probs]
            o_sel = jnp.where(low_half, o_groups[0], o_groups[1])
            return jnp.concatenate([o_sel[j * chunk:(j + 1) * chunk, :] for j in range(A_GROUP)], axis=1)

        oa_chunks = []
        s_next = scores(0)
        yield
        for c in range(n_ch):
            s_cur = s_next
            if c + 1 < n_ch:
                s_next = scores(c + 1)
            probs = [softmax(s, sk) for s, sk in zip(s_cur, sinks)]
            yield
            oa_chunks.append(values(c, probs))
            yield
        o_a = jnp.concatenate(oa_chunks, axis=0).astype(bf16) if n_ch > 1 else oa_chunks[0].astype(bf16)

        if tile >= keep:
            klo_s[0:keep, :] = klo_s[tile:tile + keep, :]
            khi_s[0:keep, :] = khi_s[tile:tile + keep, :]
            v_s[0:keep, :] = v_s[tile:tile + keep, :]
        gated("a", _dot(o_a, wpa_ref[...]))

    def task_m():
        mq = _dot(xb, w_in_ref[:, _C_MQ:_C_GT]).astype(bf16)
        yield
        heads = [slice(h * M_HEAD_DIM, (h + 1) * M_HEAD_DIM) for h in range(M_HEADS)]

        def scores(hs):
            return _dot_nt(mq[:, hs], mk_ref[0, :, hs]) * (M_HEAD_DIM ** -0.5)

        def softmax(s):
            m = jnp.max(s, axis=-1, keepdims=True)
            p = jnp.exp(s - m)
            return (p * (1.0 / jnp.sum(p, axis=-1, keepdims=True))).astype(bf16)

        om_heads = []
        for i in range(0, M_HEADS, 2):
            pair = heads[i:i + 2]
            s_pair = [scores(hs) for hs in pair]
            yield
            p_pair = [softmax(s) for s in s_pair]
            yield
            om_heads += [_dot(p, mv_ref[0, :, hs]) for p, hs in zip(p_pair, pair)]
            yield
        o_m = jnp.concatenate(om_heads, axis=1).astype(bf16)
        gated("m", _dot(o_m, wpm_ref[...]))

    def task_b():
        gk = _dot(res["gk1"], w_gk2_ref[...]) + b_gk_ref[...]
        log_a = (jnp.minimum(gk, 0.0) - jnp.log(1.0 + jnp.exp(-jnp.abs(gk)))) * (1.0 / GATE_TAU)
        yield
        gsub = min(tile, GLA_SUB)
        subs = [slice(s * gsub, (s + 1) * gsub) for s in range(tile // gsub)]
        r_i = lax.broadcasted_iota(jnp.int32, (gsub, gsub), 0)
        c_i = lax.broadcasted_iota(jnp.int32, (gsub, gsub), 1)
        if gsub > chunk:
            causal = (c_i <= r_i) & ((r_i // chunk) == (c_i // chunk))
        else:
            causal = c_i <= r_i
        tri = jnp.where(causal, 1.0, 0.0).astype(bf16)
        la_hi = log_a.astype(bf16)
        la_lo = (log_a - la_hi.astype(f32)).astype(bf16)
        cum = [_dot(tri, la_hi[rs, :]) + _dot(tri, la_lo[rs, :]) for rs in subs]
        cum = jnp.concatenate(cum, axis=0) if len(subs) > 1 else cum[0]
        last_rows = [cum[(c + 1) * chunk - 1:(c + 1) * chunk, :] for c in range(n_ch)]
        if n_ch > 1:
            last = jnp.concatenate([jnp.broadcast_to(r, (chunk, B_QK)) for r in last_rows], axis=0)
        else:
            last = last_rows[0]
        pad_rows = jnp.zeros((LANES - n_ch, B_QK), f32)
        decay_t = jnp.exp(jnp.concatenate(last_rows + [pad_rows], axis=0)).T
        yield
        bq = _dot(xb, w_in_ref[:, _C_BQ:_C_BK])
        q_dec = (bq * (B_DK ** -0.5) * jnp.exp(cum)).astype(bf16)
        yield
        bk = _dot(xb, w_in_ref[:, _C_BK:_C_BV])
        k_inv = (bk * jnp.exp(-cum)).astype(bf16)
        k_end = (bk * jnp.exp(last - cum)).astype(bf16)
        yield
        gn = gn_ref[...]
        ob_heads = []
        for h0 in range(0, B_HEADS, 2):
            pair = (h0, h0 + 1)
            ks = [slice(h * B_DK, (h + 1) * B_DK) for h in pair]
            vv = _dot(xb, w_in_ref[:, _C_BV + h0 * B_DV:_C_BV + (h0 + 2) * B_DV]).astype(bf16)
            vh = [vv[:, :B_DV], vv[:, B_DV:]]
            qd = [q_dec[:, s] for s in ks]
            ke = [k_end[:, s] for s in ks]
            att = [[jnp.where(causal, _dot_nt(qd[i][rs, :], k_inv[rs, ks[i]]), 0.0).astype(bf16) for i in range(2)]
                   for rs in subs]
            yield
            o_intra = [[_dot(att[s][i], vh[i][rs, :]) for i in range(2)] for s, rs in enumerate(subs)]
            o_intra = [jnp.concatenate([o[i] for o in o_intra], axis=0) if len(subs) > 1 else o_intra[0][i]
                       for i in range(2)]
            chunks = [slice(c * chunk, (c + 1) * chunk) for c in range(n_ch)]
            upd = [[_dot_tn(ke[i][cs, :], vh[i][cs, :]) for i in range(2)] for cs in chunks]
            state = [st_s[h] for h in pair]
            o_inter = [[], []]
            for c, cs in enumerate(chunks):
                for i in range(2):
                    o_inter[i].append(_dot(qd[i][cs, :], state[i].astype(bf16)))
                state = [state[i] * decay_t[ks[i], c:c + 1] + upd[c][i] for i in range(2)]
            for i, h in enumerate(pair):
                st_s[h] = state[i]
            yield
            gg = _dot(xb, w_in_ref[:, _C_BG + h0 * B_DV:_C_BG + (h0 + 2) * B_DV])
            for i in range(2):
                o = o_intra[i] + (jnp.concatenate(o_inter[i], axis=0) if n_ch > 1 else o_inter[i][0])
                o = o * lax.rsqrt(jnp.mean(o * o, axis=-1, keepdims=True) + RMS_EPS) * gn
                g_h = gg[:, i * B_DV:(i + 1) * B_DV]
                ob_heads.append((o * (g_h * _sigmoid(g_h))).astype(bf16))
            yield
        o_b = jnp.concatenate(ob_heads, axis=1)
        gated("b", _dot(o_b, wpb_ref[...]))

    tasks = {
        "A": task_a(), "M": task_m(), "B": task_b(),
        "ga": gate_task("a", _C_GT), "gb": gate_task("b", _C_GT + D_MODEL),
        "gm": gate_task("m", _C_GT + 2 * D_MODEL),
    }
    for name in MIXER_ORDER.split():
        next(tasks[name], None)
    for gen in tasks.values():
        for _ in gen:
            pass

    for key in "abm":
        if "gated_" + key not in res:
            res["gated_" + key] = res["gate_" + key] * res["proj_" + key]
    rb = min(tile, OUT_ROWS)
    pending = None
    for r in range(tile // rb):
        rs = slice(r * rb, (r + 1) * rb)
        merged = res["gated_a"][rs] + res["gated_b"][rs] + res["gated_m"][rs]
        z = alpha * x[rs] + _dot(merged.astype(bf16), wout_ref[...])
        if pending is not None:
            y_ref[0, pending[0], :] = _layer_norm(pending[1], ln_g_ref[...], ln_b_ref[...])
        pending = (rs, z)
    y_ref[0, pending[0], :] = _layer_norm(pending[1], ln_g_ref[...], ln_b_ref[...])

    @pl.when(t == last_t)
    def _emit_state_and_window():
        st_ref[0] = st_s[...]

        def emit(ref, lo, hi, rows):
            for g in range(A_KV_HEADS):
                ref[0, lo:hi, g, :] = rows[:, g * A_HEAD_DIM:(g + 1) * A_HEAD_DIM]

        if tile < keep:
            emit(wk_ref, 0, keep - tile, ck_ref[0, tile:keep, :])
            emit(wv_ref, 0, keep - tile, cv_ref[0, tile:keep, :])
        emit(wk_ref, max(keep - tile, 0), keep, res["k_tail"])
        emit(wv_ref, max(keep - tile, 0), keep, res["v_tail"])


def _layer_spec(stacked, layer):
    rest = stacked.shape[1:]
    index = lambda *_: (layer,) + (0,) * len(rest)
    return pl.BlockSpec((None,) + rest, index, pipeline_mode=pl.Buffered(1))


def _mixer(x, rope_tabs, mk, mv, cache, lw, layer, *, tile, chunk, alpha):
    nb, seq, _ = x.shape
    n_mem = mk.shape[2]
    keep = WINDOW
    has_cache = cache is not None
    grid = (nb, seq // tile)
    assert seq % tile == 0 and tile % chunk == 0
    assert tile >= keep or (has_cache and grid[1] == 1)

    per_b = lambda b, t: (b, 0, 0)
    layer_b = lambda b, t: (layer, b, 0, 0)
    in_specs = [
        pl.BlockSpec(memory_space=pltpu.SMEM),
        pl.BlockSpec((1, tile, D_MODEL), lambda b, t: (b, t, 0)),
        pl.BlockSpec((tile, LANES), lambda b, t: (t, 0)),
        pl.BlockSpec((tile, LANES), lambda b, t: (t, 0)),
        pl.BlockSpec((tile, LANES), lambda b, t: (t, 0)),
        pl.BlockSpec((None, 1, n_mem, M_Q), layer_b),
        pl.BlockSpec((None, 1, n_mem, M_Q), layer_b),
    ]
    args = [lw["sinks"], x, *rope_tabs, mk, mv]
    if has_cache:
        in_specs += [
            pl.BlockSpec((None, 1, keep, LANES), layer_b),
            pl.BlockSpec((None, 1, keep, LANES), layer_b),
            pl.BlockSpec((None, 1, B_HEADS, B_DK, B_DV), lambda b, t: (layer, b, 0, 0, 0)),
        ]
        args += list(cache)
    weights = [lw["w_in"], lw["w_gk2"], lw["b_gk"], lw["gn"], lw["wpa"], lw["wpb"], lw["wpm"], lw["wout"],
               lw["ln1_g"], lw["ln1_b"]]
    in_specs += [_layer_spec(w, layer) for w in weights]
    args += weights

    out_shape = (
        jax.ShapeDtypeStruct((nb, seq, D_MODEL), f32),
        jax.ShapeDtypeStruct((nb, keep, A_KV_HEADS, A_HEAD_DIM), f32),
        jax.ShapeDtypeStruct((nb, keep, A_KV_HEADS, A_HEAD_DIM), f32),
        jax.ShapeDtypeStruct((nb, B_HEADS, B_DK, B_DV), f32),
    )
    out_specs = (
        pl.BlockSpec((1, tile, D_MODEL), lambda b, t: (b, t, 0)),
        pl.BlockSpec((1, keep, A_KV_HEADS, A_HEAD_DIM), lambda b, t: (b, 0, 0, 0)),
        pl.BlockSpec((1, keep, A_KV_HEADS, A_HEAD_DIM), lambda b, t: (b, 0, 0, 0)),
        pl.BlockSpec((1, B_HEADS, B_DK, B_DV), lambda b, t: (b, 0, 0, 0)),
    )
    scratch = [
        pltpu.VMEM((keep + tile, LANES), bf16),
        pltpu.VMEM((keep + tile, LANES), bf16),
        pltpu.VMEM((keep + tile, LANES), bf16),
        pltpu.VMEM((B_HEADS, B_DK, B_DV), f32),
    ]
    kern = functools.partial(_mixer_kernel, layer=layer, tile=tile, chunk=chunk, keep=keep, has_cache=has_cache,
                             mask_start=not has_cache, alpha=alpha)
    return pl.pallas_call(
        kern,
        grid=grid,
        in_specs=in_specs,
        out_specs=out_specs,
        out_shape=out_shape,
        scratch_shapes=scratch,
        compiler_params=pltpu.CompilerParams(
            dimension_semantics=("arbitrary", "arbitrary"), vmem_limit_bytes=VMEM_LIMIT_BYTES),
        name="mixer_sample" if has_cache else "mixer_prompt",
    )(*args)


def _mlp_kernel(x_ref, wup_ref, bup_ref, wdn_ref, bdn_ref, g_ref, b_ref, y_ref, *, ff_block, sub, alpha):
    pending = None

    def finish(rows, z):
        y_ref[rows, :] = _layer_norm(z, g_ref[...], b_ref[...])

    for r in range(x_ref.shape[0] // sub):
        rows = slice(r * sub, (r + 1) * sub)
        x = x_ref[rows, :]
        xb = x.astype(bf16)
        acc = None
        for j in range(D_FF // ff_block):
            fs = slice(j * ff_block, (j + 1) * ff_block)
            h = jnp.maximum(_dot(xb, wup_ref[:, fs]) + bup_ref[:, fs], 0.0)
            part = _dot((h * h).astype(bf16), wdn_ref[fs, :])
            acc = part if acc is None else acc + part
            if j == 0 and pending is not None:
                finish(*pending)
        pending = (rows, alpha * x + (acc + bdn_ref[...]))
    finish(*pending)


def _mlp(x2d, lw, layer, *, tile, alpha):
    n = x2d.shape[0]
    assert n % tile == 0
    weights = [lw["wup"], lw["bup"], lw["wdn"], lw["bdn"], lw["ln2_g"], lw["ln2_b"]]
    return pl.pallas_call(
        functools.partial(_mlp_kernel, ff_block=1024, sub=min(tile, MLP_SUB), alpha=alpha),
        grid=(n // tile,),
        in_specs=[pl.BlockSpec((tile, D_MODEL), lambda i: (i, 0))] + [_layer_spec(w, layer) for w in weights],
        out_specs=pl.BlockSpec((tile, D_MODEL), lambda i: (i, 0)),
        out_shape=jax.ShapeDtypeStruct((n, D_MODEL), f32),
        compiler_params=pltpu.CompilerParams(
            dimension_semantics=("arbitrary",), vmem_limit_bytes=VMEM_LIMIT_BYTES),
        name="mlp",
    )(x2d, *weights)


def _memkv_kernel(mem_ref, w_ref, k_ref, v_ref, kb_ref, vb_ref):
    rows, n_mem, _ = mem_ref.shape
    kv = _dot(mem_ref[...].reshape(rows * n_mem, D_MODEL).astype(bf16), w_ref[...])
    k = kv[:, :M_Q].reshape(rows, n_mem, M_Q)
    v = kv[:, M_Q:].reshape(rows, n_mem, M_Q)
    for h in range(M_HEADS):
        hs = slice(h * M_HEAD_DIM, (h + 1) * M_HEAD_DIM)
        k_ref[:, :, h, :] = k[:, :, hs]
        v_ref[:, :, h, :] = v[:, :, hs]
    kb_ref[...] = k.astype(bf16)
    vb_ref[...] = v.astype(bf16)


def _memkv(mem, w_mem_kv_b):
    depth = w_mem_kv_b.shape[0]
    nb, n_mem, _ = mem.shape
    rows = _pick_tile(nb, 4)
    out_spec = pl.BlockSpec((None, rows, n_mem, M_Q), lambda l, b: (l, b, 0, 0))
    return pl.pallas_call(
        _memkv_kernel,
        grid=(depth, nb // rows),
        in_specs=[pl.BlockSpec((rows, n_mem, D_MODEL), lambda l, b: (b, 0, 0)),
                  pl.BlockSpec((None, D_MODEL, 2 * M_Q), lambda l, b: (l, 0, 0))],
        out_specs=(pl.BlockSpec((None, rows, n_mem, M_HEADS, M_HEAD_DIM), lambda l, b: (l, b, 0, 0, 0)),) * 2
        + (out_spec,) * 2,
        out_shape=(jax.ShapeDtypeStruct((depth, nb, n_mem, M_HEADS, M_HEAD_DIM), f32),) * 2
        + (jax.ShapeDtypeStruct((depth, nb, n_mem, M_Q), bf16),) * 2,
        compiler_params=pltpu.CompilerParams(
            dimension_semantics=("arbitrary", "arbitrary"), vmem_limit_bytes=VMEM_LIMIT_BYTES),
        name="memkv",
    )(mem, w_mem_kv_b)


def _rope_tables(pos):
    half = ROT_DIM // 2
    inv = ROPE_THETA ** (-jnp.arange(half, dtype=f32) / half)
    ang = pos.astype(f32)[:, None] * inv[None, :]
    cos, sin = jnp.cos(ang), jnp.sin(ang)
    n = pos.shape[0]
    pad = jnp.zeros((n, A_HEAD_DIM - ROT_DIM), f32)
    zero = jnp.zeros((n, half), f32)
    cos_h = jnp.concatenate([cos, cos, jnp.ones_like(pad)], axis=1)
    lo_h = jnp.concatenate([-sin, zero, pad], axis=1)
    hi_h = jnp.concatenate([zero, sin, pad], axis=1)
    return tuple(jnp.concatenate([t_, t_], axis=1) for t_ in (cos_h, lo_h, hi_h))


_HEAD_ORDER = [h for j in range(A_GROUP) for h in (j, j + A_GROUP)]
_S_BQ = A_Q + 2 * A_KV
_S_GK = _S_BQ + 2 * B_QK + 2 * B_V
_S_MQ = _S_GK + GATE_RANK
_S_END = _S_MQ + M_Q + N_BRANCH * D_MODEL


def _relayout_kernel(src_ref, dst_ref):
    def move(dst_lo, src_lo, width):
        dst_ref[:, dst_lo:dst_lo + width] = src_ref[:, src_lo:src_lo + width].astype(bf16)

    for k, h in enumerate(_HEAD_ORDER):
        move(_C_AQ + k * A_HEAD_DIM, h * A_HEAD_DIM, A_HEAD_DIM)
    move(_C_AK, A_Q, 2 * A_KV)
    dst_ref[:, _C_GK:_C_BQ] = jnp.zeros((dst_ref.shape[0], GK_PAD), bf16)
    move(_C_GK, _S_GK, GATE_RANK)
    move(_C_BQ, _S_BQ, _S_GK - _S_BQ)
    move(_C_MQ, _S_MQ, _S_END - _S_MQ)


def _relayout_w_in(w_in):
    depth, d_model, d_in = w_in.shape
    assert d_in == _S_END
    rows = _pick_tile(d_model, 128)
    return pl.pallas_call(
        _relayout_kernel,
        grid=(depth, d_model // rows),
        in_specs=[pl.BlockSpec((None, rows, d_in), lambda l, r: (l, r, 0))],
        out_specs=pl.BlockSpec((None, rows, _C_END), lambda l, r: (l, r, 0)),
        out_shape=jax.ShapeDtypeStruct((depth, d_model, _C_END), bf16),
        compiler_params=pltpu.CompilerParams(
            dimension_semantics=("arbitrary", "arbitrary"), vmem_limit_bytes=VMEM_LIMIT_BYTES),
        name="relayout_w_in",
    )(w_in)


def _prep_weights(w_in, w_gk2, b_gk, attn_sinks, gla_norm_g, w_proj_a, w_proj_b, w_proj_m, w_out,
                  ln1_g, ln1_b, w_up, b_up, w_down, b_down, ln2_g, ln2_b):
    head_order = _HEAD_ORDER
    w_in_r = _relayout_w_in(w_in)
    wpa = jnp.concatenate([w_proj_a[:, h * A_HEAD_DIM:(h + 1) * A_HEAD_DIM, :] for h in head_order], axis=1)
    row = lambda v: v[:, None, :]
    return dict(
        sinks=attn_sinks,
        w_in=w_in_r,
        w_gk2=jnp.pad(w_gk2, ((0, 0), (0, GK_PAD - GATE_RANK), (0, 0))).astype(bf16),
        b_gk=row(b_gk), gn=row(gla_norm_g),
        wpa=wpa.astype(bf16), wpb=w_proj_b.astype(bf16), wpm=w_proj_m.astype(bf16), wout=w_out.astype(bf16),
        ln1_g=row(ln1_g), ln1_b=row(ln1_b),
        wup=w_up.astype(bf16), bup=row(b_up), wdn=w_down.astype(bf16), bdn=row(b_down),
        ln2_g=row(ln2_g), ln2_b=row(ln2_b),
    )


def _pick_tile(n, pref):
    tile = min(pref, n)
    while n % tile:
        tile //= 2
    return tile


def kernel(x_prompt, x_sample, cache_win_k, cache_win_v, state_gla, cache_mem_k, cache_mem_v, mem_prompt,
           w_in, w_gk2, b_gk, attn_sinks, gla_norm_g, w_mem_kv, w_proj_a, w_proj_b, w_proj_m, w_out,
           ln1_g, ln1_b, w_up, b_up, w_down, b_down, ln2_g, ln2_b):
    depth = w_in.shape[0]
    alpha = (2 * depth) ** 0.25
    lws = _prep_weights(w_in, w_gk2, b_gk, attn_sinks, gla_norm_g, w_proj_a, w_proj_b, w_proj_m, w_out,
                        ln1_g, ln1_b, w_up, b_up, w_down, b_down, ln2_g, ln2_b)

    bp, sp, _ = x_prompt.shape
    n_mem = mem_prompt.shape[1]
    mk_p, mv_p, mk_b, mv_b = _memkv(mem_prompt, w_mem_kv.astype(bf16))
    rope_p = _rope_tables(jnp.arange(sp, dtype=jnp.int32))
    mix_tile = _pick_tile(sp, MIX_TILE)
    mlp_tile = _pick_tile(bp * sp, 1024)
    x = x_prompt
    wk_p, wv_p, gs_p = [], [], []
    for l in range(depth):
        x, wk, wv, st = _mixer(x, rope_p, mk_b, mv_b, None, lws, l, tile=mix_tile, chunk=CHUNK, alpha=alpha)
        x = _mlp(x.reshape(bp * sp, D_MODEL), lws, l, tile=mlp_tile, alpha=alpha).reshape(bp, sp, D_MODEL)
        wk_p.append(wk)
        wv_p.append(wv)
        gs_p.append(st)
    y_prompt = x

    bs, ts, _ = x_sample.shape
    keep = cache_win_k.shape[2]
    rope_s = _rope_tables(PAST_LEN + jnp.arange(ts, dtype=jnp.int32))
    cache = (cache_win_k.reshape(depth, bs, keep, A_KV), cache_win_v.reshape(depth, bs, keep, A_KV), state_gla)
    mk_s = cache_mem_k.reshape(depth, bs, n_mem, M_Q).astype(bf16)
    mv_s = cache_mem_v.reshape(depth, bs, n_mem, M_Q).astype(bf16)
    x = x_sample
    wk_s, wv_s, gs_s = [], [], []
    for l in range(depth):
        x, wk, wv, st = _mixer(x, rope_s, mk_s, mv_s, cache, lws, l, tile=ts, chunk=ts, alpha=alpha)
        x = _mlp(x.reshape(bs * ts, D_MODEL), lws, l, tile=_pick_tile(bs * ts, 512), alpha=alpha)
        x = x.reshape(bs, ts, D_MODEL)
        wk_s.append(wk)
        wv_s.append(wv)
        gs_s.append(st)
    y_sample = x

    return (y_prompt, y_sample,
            jnp.stack(wk_p), jnp.stack(wv_p), jnp.stack(gs_p),
            mk_p, mv_p,
            jnp.stack(wk_s), jnp.stack(wv_s), jnp.stack(gs_s))
```

```python
import functools

import jax
import jax.numpy as jnp
from jax import lax
from jax.experimental import pallas as pl
from jax.experimental.pallas import tpu as pltpu

D_MODEL = 1024
PAST_LEN = 1024
CHUNK = 64
WINDOW = 128
A_HEADS = 8
A_KV_HEADS = 2
A_HEAD_DIM = 64
A_GROUP = A_HEADS // A_KV_HEADS
A_SCALE = A_HEAD_DIM ** -0.5
ROT_DIM = A_HEAD_DIM // 4
ROPE_THETA = 500000.0
A_Q = A_HEADS * A_HEAD_DIM
A_KV = A_KV_HEADS * A_HEAD_DIM
B_HEADS = 4
B_DK = 128
B_DV = 256
B_QK = B_HEADS * B_DK
B_V = B_HEADS * B_DV
GATE_RANK = 16
GATE_TAU = 16.0
M_HEADS = 4
M_HEAD_DIM = 128
M_Q = M_HEADS * M_HEAD_DIM
D_FF = 4 * D_MODEL
N_BRANCH = 3
LN_EPS = 1e-5
RMS_EPS = 1e-6
NEG_INF = -1e30

LANES = 128
GK_PAD = LANES
VMEM_LIMIT_BYTES = 60000 * 1024
GATE_BLOCK = 512
MLP_SUB = 256
GLA_SUB = 256
MIX_TILE = 512
OUT_ROWS = 256
MIXER_ORDER = ("A M B A ga B A M B A ga A A M A B M A gm A M A B A M A gm A M A B A gb A B A M A B A gb"
               " B A B B")

_C_AQ = 0
_C_AK = _C_AQ + A_Q
_C_AV = _C_AK + A_KV
_C_GK = _C_AV + A_KV
_C_BQ = _C_GK + GK_PAD
_C_BK = _C_BQ + B_QK
_C_BV = _C_BK + B_QK
_C_BG = _C_BV + B_V
_C_MQ = _C_BG + B_V
_C_GT = _C_MQ + M_Q
_C_END = _C_GT + N_BRANCH * D_MODEL

_NT = (((1,), (1,)), ((), ()))
_TN = (((0,), (0,)), ((), ()))

bf16 = jnp.bfloat16
f32 = jnp.float32


def _dot(a, b):
    return jnp.dot(a, b, preferred_element_type=f32)


def _dot_nt(a, b):
    return lax.dot_general(a, b, _NT, preferred_element_type=f32)


def _dot_tn(a, b):
    return lax.dot_general(a, b, _TN, preferred_element_type=f32)


def _layer_norm(z, g, b):
    mu = jnp.mean(z, axis=-1, keepdims=True)
    zc = z - mu
    var = jnp.mean(zc * zc, axis=-1, keepdims=True)
    return zc * lax.rsqrt(var + LN_EPS) * g + b


def _sigmoid(z):
    return 0.5 * jnp.tanh(0.5 * z) + 0.5


def _mixer_kernel(*refs, layer, tile, chunk, keep, has_cache, mask_start, alpha):
    it = iter(refs)
    sinks_ref = next(it)
    x_ref = next(it)
    cos_ref, sin_lo_ref, sin_hi_ref = next(it), next(it), next(it)
    mk_ref, mv_ref = next(it), next(it)
    if has_cache:
        ck_ref, cv_ref, s0_ref = next(it), next(it), next(it)
    w_in_ref, w_gk2_ref, b_gk_ref, gn_ref = next(it), next(it), next(it), next(it)
    wpa_ref, wpb_ref, wpm_ref, wout_ref = next(it), next(it), next(it), next(it)
    ln_g_ref, ln_b_ref = next(it), next(it)
    y_ref, wk_ref, wv_ref, st_ref = next(it), next(it), next(it), next(it)
    klo_s, khi_s, v_s, st_s = next(it), next(it), next(it), next(it)

    t = pl.program_id(1)
    last_t = pl.num_programs(1) - 1
    n_ch = tile // chunk
    win = keep + chunk

    lane = lax.broadcasted_iota(jnp.int32, (1, LANES), 1)
    low_half = lane < A_HEAD_DIM

    @pl.when(t == 0)
    def _init():
        if has_cache:
            ck = ck_ref[0]
            klo_s[0:keep, :] = jnp.where(low_half, ck, 0.0).astype(bf16)
            khi_s[0:keep, :] = jnp.where(low_half, 0.0, ck).astype(bf16)
            v_s[0:keep, :] = cv_ref[0].astype(bf16)
            for h in range(B_HEADS):
                st_s[h] = s0_ref[0, h]
        else:
            zero = jnp.zeros((keep, LANES), bf16)
            klo_s[0:keep, :] = zero
            khi_s[0:keep, :] = zero
            v_s[0:keep, :] = zero
            st_s[...] = jnp.zeros(st_s.shape, f32)

    x = x_ref[0]
    xb = x.astype(bf16)
    res = {}

    def gated(key, proj):
        if "gate_" + key in res:
            res["gated_" + key] = res.pop("gate_" + key) * proj
        else:
            res["proj_" + key] = proj

    def gate_task(key, col):
        blocks = []
        n_blocks = D_MODEL // GATE_BLOCK
        for j in range(n_blocks):
            cs = slice(col + j * GATE_BLOCK, col + (j + 1) * GATE_BLOCK)
            blocks.append(_sigmoid(_dot(xb, w_in_ref[:, cs])))
            if j == n_blocks - 1:
                gate = jnp.concatenate(blocks, axis=1) if n_blocks > 1 else blocks[0]
                if "proj_" + key in res:
                    res["gated_" + key] = gate * res.pop("proj_" + key)
                else:
                    res["gate_" + key] = gate
            yield

    def task_a():
        cos = cos_ref[...]
        sin_lo = sin_lo_ref[...]
        sin_hi = sin_hi_ref[...]

        def rope(z):
            return (z * cos + pltpu.roll(z, LANES - ROT_DIM // 2, 1) * sin_lo
                    + pltpu.roll(z, ROT_DIM // 2, 1) * sin_hi)

        qkv = _dot(xb, w_in_ref[:, _C_AQ:_C_BQ])
        res["gk1"] = qkv[:, _C_GK:_C_BQ].astype(bf16)
        yield
        q_cols = [(rope(qkv[:, j * LANES:(j + 1) * LANES]) * A_SCALE).astype(bf16) for j in range(A_GROUP)]
        k_r = rope(qkv[:, _C_AK:_C_AV])
        v_new = qkv[:, _C_AV:_C_GK]
        klo_s[keep:keep + tile, :] = jnp.where(low_half, k_r, 0.0).astype(bf16)
        khi_s[keep:keep + tile, :] = jnp.where(low_half, 0.0, k_r).astype(bf16)
        v_s[keep:keep + tile, :] = v_new.astype(bf16)

        res["k_tail"] = k_r[max(tile - keep, 0):, :]
        res["v_tail"] = v_new[max(tile - keep, 0):, :]
        yield

        rows = lax.broadcasted_iota(jnp.int32, (A_GROUP * chunk, 1), 0)
        key_col = lax.broadcasted_iota(jnp.int32, (1, win), 1)
        sinks = []
        for g in range(A_KV_HEADS):
            sk = jnp.full((A_GROUP * chunk, 1), sinks_ref[layer, g * A_GROUP + A_GROUP - 1], f32)
            for j in range(A_GROUP - 2, -1, -1):
                sk = jnp.where(rows < (j + 1) * chunk, sinks_ref[layer, g * A_GROUP + j], sk)
            sinks.append(sk)

        def scores(c):
            q_st = jnp.concatenate([qc[c * chunk:(c + 1) * chunk, :] for qc in q_cols], axis=0)
            out = []
            for k_ref in (klo_s, khi_s):
                s = _dot_nt(q_st, k_ref[c * chunk:c * chunk + win, :])
                if mask_start and c * chunk < keep:
                    first_valid = keep - (t * tile + c * chunk)
                    s = jnp.where(key_col >= first_valid, s, NEG_INF)
                out.append(s)
            return out

        def softmax(s, sk):
            m = jnp.maximum(jnp.max(s, axis=-1, keepdims=True), sk)
            p = jnp.exp(s - m)
            den = jnp.sum(p, axis=-1, keepdims=True) + jnp.exp(sk - m)
            return (p * (1.0 / den)).astype(bf16)

        def values(c, probs):
            o_groups = [_dot(p, v_s[c * chunk:c * chunk + win, :]) for p in probs]
            o_sel = jnp.where(low_half, o_groups[0], o_groups[1])
            return jnp.concatenate([o_sel[j * chunk:(j + 1) * chunk, :] for j in range(A_GROUP)], axis=1)

        oa_chunks = []
        s_next = scores(0)
        yield
        for c in range(n_ch):
            s_cur = s_next
            if c + 1 < n_ch:
                s_next = scores(c + 1)
            probs = [softmax(s, sk) for s, sk in zip(s_cur, sinks)]
            yield
            oa_chunks.append(values(c, probs))
            yield
        o_a = jnp.concatenate(oa_chunks, axis=0).astype(bf16) if n_ch > 1 else oa_chunks[0].astype(bf16)

        if tile >= keep:
            klo_s[0:keep, :] = klo_s[tile:tile + keep, :]
            khi_s[0:keep, :] = khi_s[tile:tile + keep, :]
            v_s[0:keep, :] = v_s[tile:tile + keep, :]
        gated("a", _dot(o_a, wpa_ref[...]))

    def task_m():
        mq = _dot(xb, w_in_ref[:, _C_MQ:_C_GT]).astype(bf16)
        yield
        heads = [slice(h * M_HEAD_DIM, (h + 1) * M_HEAD_DIM) for h in range(M_HEADS)]

        def scores(hs):
            return _dot_nt(mq[:, hs], mk_ref[0, :, hs]) * (M_HEAD_DIM ** -0.5)

        def softmax(s):
            m = jnp.max(s, axis=-1, keepdims=True)
            p = jnp.exp(s - m)
            return (p * (1.0 / jnp.sum(p, axis=-1, keepdims=True))).astype(bf16)

        om_heads = []
        for i in range(0, M_HEADS, 2):
            pair = heads[i:i + 2]
            s_pair = [scores(hs) for hs in pair]
            yield
            p_pair = [softmax(s) for s in s_pair]
            yield
            om_heads += [_dot(p, mv_ref[0, :, hs]) for p, hs in zip(p_pair, pair)]
            yield
        o_m = jnp.concatenate(om_heads, axis=1).astype(bf16)
        gated("m", _dot(o_m, wpm_ref[...]))

    def task_b():
        gk = _dot(res["gk1"], w_gk2_ref[...]) + b_gk_ref[...]
        log_a = (jnp.minimum(gk, 0.0) - jnp.log(1.0 + jnp.exp(-jnp.abs(gk)))) * (1.0 / GATE_TAU)
        yield
        gsub = min(tile, GLA_SUB)
        subs = [slice(s * gsub, (s + 1) * gsub) for s in range(tile // gsub)]
        r_i = lax.broadcasted_iota(jnp.int32, (gsub, gsub), 0)
        c_i = lax.broadcasted_iota(jnp.int32, (gsub, gsub), 1)
        if gsub > chunk:
            causal = (c_i <= r_i) & ((r_i // chunk) == (c_i // chunk))
        else:
            causal = c_i <= r_i
        tri = jnp.where(causal, 1.0, 0.0).astype(bf16)
        la_hi = log_a.astype(bf16)
        la_lo = (log_a - la_hi.astype(f32)).astype(bf16)
        cum = [_dot(tri, la_hi[rs, :]) + _dot(tri, la_lo[rs, :]) for rs in subs]
        cum = jnp.concatenate(cum, axis=0) if len(subs) > 1 else cum[0]
        last_rows = [cum[(c + 1) * chunk - 1:(c + 1) * chunk, :] for c in range(n_ch)]
        if n_ch > 1:
            last = jnp.concatenate([jnp.broadcast_to(r, (chunk, B_QK)) for r in last_rows], axis=0)
        else:
            last = last_rows[0]
        pad_rows = jnp.zeros((LANES - n_ch, B_QK), f32)
        decay_t = jnp.exp(jnp.concatenate(last_rows + [pad_rows], axis=0)).T
        yield
        bq = _dot(xb, w_in_ref[:, _C_BQ:_C_BK])
        q_dec = (bq * (B_DK ** -0.5) * jnp.exp(cum)).astype(bf16)
        yield
        bk = _dot(xb, w_in_ref[:, _C_BK:_C_BV])
        k_inv = (bk * jnp.exp(-cum)).astype(bf16)
        k_end = (bk * jnp.exp(last - cum)).astype(bf16)
        yield
        gn = gn_ref[...]
        ob_heads = []
        for h0 in range(0, B_HEADS, 2):
            pair = (h0, h0 + 1)
            ks = [slice(h * B_DK, (h + 1) * B_DK) for h in pair]
            vv = _dot(xb, w_in_ref[:, _C_BV + h0 * B_DV:_C_BV + (h0 + 2) * B_DV]).astype(bf16)
            vh = [vv[:, :B_DV], vv[:, B_DV:]]
            qd = [q_dec[:, s] for s in ks]
            ke = [k_end[:, s] for s in ks]
            att = [[jnp.where(causal, _dot_nt(qd[i][rs, :], k_inv[rs, ks[i]]), 0.0).astype(bf16) for i in range(2)]
                   for rs in subs]
            yield
            o_intra = [[_dot(att[s][i], vh[i][rs, :]) for i in range(2)] for s, rs in enumerate(subs)]
            o_intra = [jnp.concatenate([o[i] for o in o_intra], axis=0) if len(subs) > 1 else o_intra[0][i]
                       for i in range(2)]
            chunks = [slice(c * chunk, (c + 1) * chunk) for c in range(n_ch)]
            upd = [[_dot_tn(ke[i][cs, :], vh[i][cs, :]) for i in range(2)] for cs in chunks]
            state = [st_s[h] for h in pair]
            o_inter = [[], []]
            for c, cs in enumerate(chunks):
                for i in range(2):
                    o_inter[i].append(_dot(qd[i][cs, :], state[i].astype(bf16)))
                state = [state[i] * decay_t[ks[i], c:c + 1] + upd[c][i] for i in range(2)]
            for i, h in enumerate(pair):
                st_s[h] = state[i]
            yield
            gg = _dot(xb, w_in_ref[:, _C_BG + h0 * B_DV:_C_BG + (h0 + 2) * B_DV])
            for i in range(2):
                o = o_intra[i] + (jnp.concatenate(o_inter[i], axis=0) if n_ch > 1 else o_inter[i][0])
                o = o * lax.rsqrt(jnp.mean(o * o, axis=-1, keepdims=True) + RMS_EPS) * gn
                g_h = gg[:, i * B_DV:(i + 1) * B_DV]
                ob_heads.append((o * (g_h * _sigmoid(g_h))).astype(bf16))
            yield
        o_b = jnp.concatenate(ob_heads, axis=1)
        gated("b", _dot(o_b, wpb_ref[...]))

    tasks = {
        "A": task_a(), "M": task_m(), "B": task_b(),
        "ga": gate_task("a", _C_GT), "gb": gate_task("b", _C_GT + D_MODEL),
        "gm": gate_task("m", _C_GT + 2 * D_MODEL),
    }
    for name in MIXER_ORDER.split():
        next(tasks[name], None)
    for gen in tasks.values():
        for _ in gen:
            pass

    for key in "abm":
        if "gated_" + key not in res:
            res["gated_" + key] = res["gate_" + key] * res["proj_" + key]
    rb = min(tile, OUT_ROWS)
    pending = None
    for r in range(tile // rb):
        rs = slice(r * rb, (r + 1) * rb)
        merged = res["gated_a"][rs] + res["gated_b"][rs] + res["gated_m"][rs]
        z = alpha * x[rs] + _dot(merged.astype(bf16), wout_ref[...])
        if pending is not None:
            y_ref[0, pending[0], :] = _layer_norm(pending[1], ln_g_ref[...], ln_b_ref[...])
        pending = (rs, z)
    y_ref[0, pending[0], :] = _layer_norm(pending[1], ln_g_ref[...], ln_b_ref[...])

    @pl.when(t == last_t)
    def _emit_state_and_window():
        st_ref[0] = st_s[...]

        def emit(ref, lo, hi, rows):
            for g in range(A_KV_HEADS):
                ref[0, lo:hi, g, :] = rows[:, g * A_HEAD_DIM:(g + 1) * A_HEAD_DIM]

        if tile < keep:
            emit(wk_ref, 0, keep - tile, ck_ref[0, tile:keep, :])
            emit(wv_ref, 0, keep - tile, cv_ref[0, tile:keep, :])
        emit(wk_ref, max(keep - tile, 0), keep, res["k_tail"])
        emit(wv_ref, max(keep - tile, 0), keep, res["v_tail"])


def _layer_spec(stacked, layer):
    rest = stacked.shape[1:]
    index = lambda *_: (layer,) + (0,) * len(rest)
    return pl.BlockSpec((None,) + rest, index, pipeline_mode=pl.Buffered(1))


def _mixer(x, rope_tabs, mk, mv, cache, lw, layer, *, tile, chunk, alpha):
    nb, seq, _ = x.shape
    n_mem = mk.shape[2]
    keep = WINDOW
    has_cache = cache is not None
    grid = (nb, seq // tile)
    assert seq % tile == 0 and tile % chunk == 0
    assert tile >= keep or (has_cache and grid[1] == 1)

    per_b = lambda b, t: (b, 0, 0)
    layer_b = lambda b, t: (layer, b, 0, 0)
    in_specs = [
        pl.BlockSpec(memory_space=pltpu.SMEM),
        pl.BlockSpec((1, tile, D_MODEL), lambda b, t: (b, t, 0)),
        pl.BlockSpec((tile, LANES), lambda b, t: (t, 0)),
        pl.BlockSpec((tile, LANES), lambda b, t: (t, 0)),
        pl.BlockSpec((tile, LANES), lambda b, t: (t, 0)),
        pl.BlockSpec((None, 1, n_mem, M_Q), layer_b),
        pl.BlockSpec((None, 1, n_mem, M_Q), layer_b),
    ]
    args = [lw["sinks"], x, *rope_tabs, mk, mv]
    if has_cache:
        in_specs += [
            pl.BlockSpec((None, 1, keep, LANES), layer_b),
            pl.BlockSpec((None, 1, keep, LANES), layer_b),
            pl.BlockSpec((None, 1, B_HEADS, B_DK, B_DV), lambda b, t: (layer, b, 0, 0, 0)),
        ]
        args += list(cache)
    weights = [lw["w_in"], lw["w_gk2"], lw["b_gk"], lw["gn"], lw["wpa"], lw["wpb"], lw["wpm"], lw["wout"],
               lw["ln1_g"], lw["ln1_b"]]
    in_specs += [_layer_spec(w, layer) for w in weights]
    args += weights

    out_shape = (
        jax.ShapeDtypeStruct((nb, seq, D_MODEL), f32),
        jax.ShapeDtypeStruct((nb, keep, A_KV_HEADS, A_HEAD_DIM), f32),
        jax.ShapeDtypeStruct((nb, keep, A_KV_HEADS, A_HEAD_DIM), f32),
        jax.ShapeDtypeStruct((nb, B_HEADS, B_DK, B_DV), f32),
    )
    out_specs = (
        pl.BlockSpec((1, tile, D_MODEL), lambda b, t: (b, t, 0)),
        pl.BlockSpec((1, keep, A_KV_HEADS, A_HEAD_DIM), lambda b, t: (b, 0, 0, 0)),
        pl.BlockSpec((1, keep, A_KV_HEADS, A_HEAD_DIM), lambda b, t: (b, 0, 0, 0)),
        pl.BlockSpec((1, B_HEADS, B_DK, B_DV), lambda b, t: (b, 0, 0, 0)),
    )
    scratch = [
        pltpu.VMEM((keep + tile, LANES), bf16),
        pltpu.VMEM((keep + tile, LANES), bf16),
        pltpu.VMEM((keep + tile, LANES), bf16),
        pltpu.VMEM((B_HEADS, B_DK, B_DV), f32),
    ]
    kern = functools.partial(_mixer_kernel, layer=layer, tile=tile, chunk=chunk, keep=keep, has_cache=has_cache,
                             mask_start=not has_cache, alpha=alpha)
    return pl.pallas_call(
        kern,
        grid=grid,
        in_specs=in_specs,
        out_specs=out_specs,
        out_shape=out_shape,
        scratch_shapes=scratch,
        compiler_params=pltpu.CompilerParams(
            dimension_semantics=("arbitrary", "arbitrary"), vmem_limit_bytes=VMEM_LIMIT_BYTES),
        name="mixer_sample" if has_cache else "mixer_prompt",
    )(*args)


def _mlp_kernel(x_ref, wup_ref, bup_ref, wdn_ref, bdn_ref, g_ref, b_ref, y_ref, *, ff_block, sub, alpha):
    pending = None

    def finish(rows, z):
        y_ref[rows, :] = _layer_norm(z, g_ref[...], b_ref[...])

    for r in range(x_ref.shape[0] // sub):
        rows = slice(r * sub, (r + 1) * sub)
        x = x_ref[rows, :]
        xb = x.astype(bf16)
        acc = None
        for j in range(D_FF // ff_block):
            fs = slice(j * ff_block, (j + 1) * ff_block)
            h = jnp.maximum(_dot(xb, wup_ref[:, fs]) + bup_ref[:, fs], 0.0)
            part = _dot((h * h).astype(bf16), wdn_ref[fs, :])
            acc = part if acc is None else acc + part
            if j == 0 and pending is not None:
                finish(*pending)
        pending = (rows, alpha * x + (acc + bdn_ref[...]))
    finish(*pending)


def _mlp(x2d, lw, layer, *, tile, alpha):
    n = x2d.shape[0]
    assert n % tile == 0
    weights = [lw["wup"], lw["bup"], lw["wdn"], lw["bdn"], lw["ln2_g"], lw["ln2_b"]]
    return pl.pallas_call(
        functools.partial(_mlp_kernel, ff_block=1024, sub=min(tile, MLP_SUB), alpha=alpha),
        grid=(n // tile,),
        in_specs=[pl.BlockSpec((tile, D_MODEL), lambda i: (i, 0))] + [_layer_spec(w, layer) for w in weights],
        out_specs=pl.BlockSpec((tile, D_MODEL), lambda i: (i, 0)),
        out_shape=jax.ShapeDtypeStruct((n, D_MODEL), f32),
        compiler_params=pltpu.CompilerParams(
            dimension_semantics=("arbitrary",), vmem_limit_bytes=VMEM_LIMIT_BYTES),
        name="mlp",
    )(x2d, *weights)


def _memkv_kernel(mem_ref, w_ref, k_ref, v_ref, kb_ref, vb_ref):
    rows, n_mem, _ = mem_ref.shape
    kv = _dot(mem_ref[...].reshape(rows * n_mem, D_MODEL).astype(bf16), w_ref[...])
    k = kv[:, :M_Q]
    v = kv[:, M_Q:]
    for h in range(M_HEADS):
        hs = slice(h * M_HEAD_DIM, (h + 1) * M_HEAD_DIM)
        k_ref[pl.ds(h, rows * n_mem, stride=M_HEADS), :] = k[:, hs]
        v_ref[pl.ds(h, rows * n_mem, stride=M_HEADS), :] = v[:, hs]
    kb_ref[...] = k.reshape(rows, n_mem, M_Q).astype(bf16)
    vb_ref[...] = v.reshape(rows, n_mem, M_Q).astype(bf16)


def _memkv(mem, w_mem_kv_b):
    depth = w_mem_kv_b.shape[0]
    nb, n_mem, _ = mem.shape
    rows = _pick_tile(nb, 4)
    out_spec = pl.BlockSpec((None, rows, n_mem, M_Q), lambda l, b: (l, b, 0, 0))
    return pl.pallas_call(
        _memkv_kernel,
        grid=(depth, nb // rows),
        in_specs=[pl.BlockSpec((rows, n_mem, D_MODEL), lambda l, b: (b, 0, 0)),
                  pl.BlockSpec((None, D_MODEL, 2 * M_Q), lambda l, b: (l, 0, 0))],
        out_specs=(pl.BlockSpec((None, rows * n_mem * M_HEADS, M_HEAD_DIM), lambda l, b: (l, b, 0)),) * 2
        + (out_spec,) * 2,
        out_shape=(jax.ShapeDtypeStruct((depth, nb * n_mem * M_HEADS, M_HEAD_DIM), f32),) * 2
        + (jax.ShapeDtypeStruct((depth, nb, n_mem, M_Q), bf16),) * 2,
        compiler_params=pltpu.CompilerParams(
            dimension_semantics=("arbitrary", "arbitrary"), vmem_limit_bytes=VMEM_LIMIT_BYTES),
        name="memkv",
    )(mem, w_mem_kv_b)


def _rope_tables(pos):
    half = ROT_DIM // 2
    inv = ROPE_THETA ** (-jnp.arange(half, dtype=f32) / half)
    ang = pos.astype(f32)[:, None] * inv[None, :]
    cos, sin = jnp.cos(ang), jnp.sin(ang)
    n = pos.shape[0]
    pad = jnp.zeros((n, A_HEAD_DIM - ROT_DIM), f32)
    zero = jnp.zeros((n, half), f32)
    cos_h = jnp.concatenate([cos, cos, jnp.ones_like(pad)], axis=1)
    lo_h = jnp.concatenate([-sin, zero, pad], axis=1)
    hi_h = jnp.concatenate([zero, sin, pad], axis=1)
    return tuple(jnp.concatenate([t_, t_], axis=1) for t_ in (cos_h, lo_h, hi_h))


_HEAD_ORDER = [h for j in range(A_GROUP) for h in (j, j + A_GROUP)]
_S_BQ = A_Q + 2 * A_KV
_S_GK = _S_BQ + 2 * B_QK + 2 * B_V
_S_MQ = _S_GK + GATE_RANK
_S_END = _S_MQ + M_Q + N_BRANCH * D_MODEL


def _relayout_kernel(src_ref, dst_ref):
    def move(dst_lo, src_lo, width):
        dst_ref[:, dst_lo:dst_lo + width] = src_ref[:, src_lo:src_lo + width].astype(bf16)

    for k, h in enumerate(_HEAD_ORDER):
        move(_C_AQ + k * A_HEAD_DIM, h * A_HEAD_DIM, A_HEAD_DIM)
    move(_C_AK, A_Q, 2 * A_KV)
    dst_ref[:, _C_GK:_C_BQ] = jnp.zeros((dst_ref.shape[0], GK_PAD), bf16)
    move(_C_GK, _S_GK, GATE_RANK)
    move(_C_BQ, _S_BQ, _S_GK - _S_BQ)
    move(_C_MQ, _S_MQ, _S_END - _S_MQ)


def _relayout_w_in(w_in):
    depth, d_model, d_in = w_in.shape
    assert d_in == _S_END
    rows = _pick_tile(d_model, 128)
    return pl.pallas_call(
        _relayout_kernel,
        grid=(depth, d_model // rows),
        in_specs=[pl.BlockSpec((None, rows, d_in), lambda l, r: (l, r, 0))],
        out_specs=pl.BlockSpec((None, rows, _C_END), lambda l, r: (l, r, 0)),
        out_shape=jax.ShapeDtypeStruct((depth, d_model, _C_END), bf16),
        compiler_params=pltpu.CompilerParams(
            dimension_semantics=("arbitrary", "arbitrary"), vmem_limit_bytes=VMEM_LIMIT_BYTES),
        name="relayout_w_in",
    )(w_in)


def _prep_weights(w_in, w_gk2, b_gk, attn_sinks, gla_norm_g, w_proj_a, w_proj_b, w_proj_m, w_out,
                  ln1_g, ln1_b, w_up, b_up, w_down, b_down, ln2_g, ln2_b):
    head_order = _HEAD_ORDER
    w_in_r = _relayout_w_in(w_in)
    wpa = jnp.concatenate([w_proj_a[:, h * A_HEAD_DIM:(h + 1) * A_HEAD_DIM, :] for h in head_order], axis=1)
    row = lambda v: v[:, None, :]
    return dict(
        sinks=attn_sinks,
        w_in=w_in_r,
        w_gk2=jnp.pad(w_gk2, ((0, 0), (0, GK_PAD - GATE_RANK), (0, 0))).astype(bf16),
        b_gk=row(b_gk), gn=row(gla_norm_g),
        wpa=wpa.astype(bf16), wpb=w_proj_b.astype(bf16), wpm=w_proj_m.astype(bf16), wout=w_out.astype(bf16),
        ln1_g=row(ln1_g), ln1_b=row(ln1_b),
        wup=w_up.astype(bf16), bup=row(b_up), wdn=w_down.astype(bf16), bdn=row(b_down),
        ln2_g=row(ln2_g), ln2_b=row(ln2_b),
    )


def _pick_tile(n, pref):
    tile = min(pref, n)
    while n % tile:
        tile //= 2
    return tile


def kernel(x_prompt, x_sample, cache_win_k, cache_win_v, state_gla, cache_mem_k, cache_mem_v, mem_prompt,
           w_in, w_gk2, b_gk, attn_sinks, gla_norm_g, w_mem_kv, w_proj_a, w_proj_b, w_proj_m, w_out,
           ln1_g, ln1_b, w_up, b_up, w_down, b_down, ln2_g, ln2_b):
    depth = w_in.shape[0]
    alpha = (2 * depth) ** 0.25
    lws = _prep_weights(w_in, w_gk2, b_gk, attn_sinks, gla_norm_g, w_proj_a, w_proj_b, w_proj_m, w_out,
                        ln1_g, ln1_b, w_up, b_up, w_down, b_down, ln2_g, ln2_b)

    bp, sp, _ = x_prompt.shape
    n_mem = mem_prompt.shape[1]
    mk_p, mv_p, mk_b, mv_b = _memkv(mem_prompt, w_mem_kv.astype(bf16))
    rope_p = _rope_tables(jnp.arange(sp, dtype=jnp.int32))
    mix_tile = _pick_tile(sp, MIX_TILE)
    mlp_tile = _pick_tile(bp * sp, 1024)
    x = x_prompt
    wk_p, wv_p, gs_p = [], [], []
    for l in range(depth):
        x, wk, wv, st = _mixer(x, rope_p, mk_b, mv_b, None, lws, l, tile=mix_tile, chunk=CHUNK, alpha=alpha)
        x = _mlp(x.reshape(bp * sp, D_MODEL), lws, l, tile=mlp_tile, alpha=alpha).reshape(bp, sp, D_MODEL)
        wk_p.append(wk)
        wv_p.append(wv)
        gs_p.append(st)
    y_prompt = x

    bs, ts, _ = x_sample.shape
    keep = cache_win_k.shape[2]
    rope_s = _rope_tables(PAST_LEN + jnp.arange(ts, dtype=jnp.int32))
    cache = (cache_win_k.reshape(depth, bs, keep, A_KV), cache_win_v.reshape(depth, bs, keep, A_KV), state_gla)
    mk_s = cache_mem_k.reshape(depth, bs, n_mem, M_Q).astype(bf16)
    mv_s = cache_mem_v.reshape(depth, bs, n_mem, M_Q).astype(bf16)
    x = x_sample
    wk_s, wv_s, gs_s = [], [], []
    for l in range(depth):
        x, wk, wv, st = _mixer(x, rope_s, mk_s, mv_s, cache, lws, l, tile=ts, chunk=ts, alpha=alpha)
        x = _mlp(x.reshape(bs * ts, D_MODEL), lws, l, tile=_pick_tile(bs * ts, 512), alpha=alpha)
        x = x.reshape(bs, ts, D_MODEL)
        wk_s.append(wk)
        wv_s.append(wv)
        gs_s.append(st)
    y_sample = x

    mem_shape = (depth, bp, n_mem, M_HEADS, M_HEAD_DIM)
    return (y_prompt, y_sample,
            jnp.stack(wk_p), jnp.stack(wv_p), jnp.stack(gs_p),
            mk_p.reshape(mem_shape), mv_p.reshape(mem_shape),
            jnp.stack(wk_s), jnp.stack(wv_s), jnp.stack(gs_s))
```

```python
import functools

import jax
import jax.numpy as jnp
from jax import lax
from jax.experimental import pallas as pl
from jax.experimental.pallas import tpu as pltpu

D_MODEL = 1024
PAST_LEN = 1024
CHUNK = 64
WINDOW = 128
A_HEADS = 8
A_KV_HEADS = 2
A_HEAD_DIM = 64
A_GROUP = A_HEADS // A_KV_HEADS
A_SCALE = A_HEAD_DIM ** -0.5
ROT_DIM = A_HEAD_DIM // 4
ROPE_THETA = 500000.0
A_Q = A_HEADS * A_HEAD_DIM
A_KV = A_KV_HEADS * A_HEAD_DIM
B_HEADS = 4
B_DK = 128
B_DV = 256
B_QK = B_HEADS * B_DK
B_V = B_HEADS * B_DV
GATE_RANK = 16
GATE_TAU = 16.0
M_HEADS = 4
M_HEAD_DIM = 128
M_Q = M_HEADS * M_HEAD_DIM
D_FF = 4 * D_MODEL
N_BRANCH = 3
LN_EPS = 1e-5
RMS_EPS = 1e-6
NEG_INF = -1e30

LANES = 128
GK_PAD = LANES
VMEM_LIMIT_BYTES = 60000 * 1024
GATE_BLOCK = 512
MLP_SUB = 256
GLA_SUB = 256
MIX_TILE = 512
OUT_ROWS = 256
MIXER_ORDER = ("A M B A ga B A M B A ga A A M A B M A gm A M A B A M A gm A M A B A gb A B A M A B A gb"
               " B A B B")

_C_AQ = 0
_C_AK = _C_AQ + A_Q
_C_AV = _C_AK + A_KV
_C_GK = _C_AV + A_KV
_C_BQ = _C_GK + GK_PAD
_C_BK = _C_BQ + B_QK
_C_BV = _C_BK + B_QK
_C_BG = _C_BV + B_V
_C_MQ = _C_BG + B_V
_C_GT = _C_MQ + M_Q
_C_END = _C_GT + N_BRANCH * D_MODEL

_NT = (((1,), (1,)), ((), ()))
_TN = (((0,), (0,)), ((), ()))

bf16 = jnp.bfloat16
f32 = jnp.float32


def _dot(a, b):
    return jnp.dot(a, b, preferred_element_type=f32)


def _dot_nt(a, b):
    return lax.dot_general(a, b, _NT, preferred_element_type=f32)


def _dot_tn(a, b):
    return lax.dot_general(a, b, _TN, preferred_element_type=f32)


def _layer_norm(z, g, b):
    mu = jnp.mean(z, axis=-1, keepdims=True)
    zc = z - mu
    var = jnp.mean(zc * zc, axis=-1, keepdims=True)
    return zc * lax.rsqrt(var + LN_EPS) * g + b


def _sigmoid(z):
    return 0.5 * jnp.tanh(0.5 * z) + 0.5


def _mixer_kernel(*refs, layer, tile, chunk, keep, has_cache, has_prev, mask_start, alpha):
    it = iter(refs)
    sinks_ref = next(it)
    x_ref = next(it)
    cos_ref, sin_lo_ref, sin_hi_ref = next(it), next(it), next(it)
    mk_ref, mv_ref = next(it), next(it)
    if has_cache:
        ck_ref, cv_ref, s0_ref = next(it), next(it), next(it)
    w_in_ref, w_gk2_ref, b_gk_ref, gn_ref = next(it), next(it), next(it), next(it)
    wpa_ref, wpb_ref, wpm_ref, wout_ref = next(it), next(it), next(it), next(it)
    ln_g_ref, ln_b_ref = next(it), next(it)
    if has_prev:
        next(it)
    y_ref, wk_ref, wv_ref, st_ref = next(it), next(it), next(it), next(it)
    klo_s, khi_s, v_s, st_s = next(it), next(it), next(it), next(it)

    t = pl.program_id(1)
    last_t = pl.num_programs(1) - 1
    n_ch = tile // chunk
    win = keep + chunk

    lane = lax.broadcasted_iota(jnp.int32, (1, LANES), 1)
    low_half = lane < A_HEAD_DIM

    @pl.when(t == 0)
    def _init():
        if has_cache:
            ck = ck_ref[0]
            klo_s[0:keep, :] = jnp.where(low_half, ck, 0.0).astype(bf16)
            khi_s[0:keep, :] = jnp.where(low_half, 0.0, ck).astype(bf16)
            v_s[0:keep, :] = cv_ref[0].astype(bf16)
            for h in range(B_HEADS):
                st_s[h] = s0_ref[0, h]
        else:
            zero = jnp.zeros((keep, LANES), bf16)
            klo_s[0:keep, :] = zero
            khi_s[0:keep, :] = zero
            v_s[0:keep, :] = zero
            st_s[...] = jnp.zeros(st_s.shape, f32)

    x = x_ref[0]
    xb = x.astype(bf16)
    res = {}

    def gated(key, proj):
        if "gate_" + key in res:
            res["gated_" + key] = res.pop("gate_" + key) * proj
        else:
            res["proj_" + key] = proj

    def gate_task(key, col):
        blocks = []
        n_blocks = D_MODEL // GATE_BLOCK
        for j in range(n_blocks):
            cs = slice(col + j * GATE_BLOCK, col + (j + 1) * GATE_BLOCK)
            blocks.append(_sigmoid(_dot(xb, w_in_ref[:, cs])))
            if j == n_blocks - 1:
                gate = jnp.concatenate(blocks, axis=1) if n_blocks > 1 else blocks[0]
                if "proj_" + key in res:
                    res["gated_" + key] = gate * res.pop("proj_" + key)
                else:
                    res["gate_" + key] = gate
            yield

    def task_a():
        cos = cos_ref[...]
        sin_lo = sin_lo_ref[...]
        sin_hi = sin_hi_ref[...]

        def rope(z):
            return (z * cos + pltpu.roll(z, LANES - ROT_DIM // 2, 1) * sin_lo
                    + pltpu.roll(z, ROT_DIM // 2, 1) * sin_hi)

        qkv = _dot(xb, w_in_ref[:, _C_AQ:_C_BQ])
        res["gk1"] = qkv[:, _C_GK:_C_BQ].astype(bf16)
        yield
        q_cols = [(rope(qkv[:, j * LANES:(j + 1) * LANES]) * A_SCALE).astype(bf16) for j in range(A_GROUP)]
        k_r = rope(qkv[:, _C_AK:_C_AV])
        v_new = qkv[:, _C_AV:_C_GK]
        klo_s[keep:keep + tile, :] = jnp.where(low_half, k_r, 0.0).astype(bf16)
        khi_s[keep:keep + tile, :] = jnp.where(low_half, 0.0, k_r).astype(bf16)
        v_s[keep:keep + tile, :] = v_new.astype(bf16)

        res["k_tail"] = k_r[max(tile - keep, 0):, :]
        res["v_tail"] = v_new[max(tile - keep, 0):, :]
        yield

        rows = lax.broadcasted_iota(jnp.int32, (A_GROUP * chunk, 1), 0)
        key_col = lax.broadcasted_iota(jnp.int32, (1, win), 1)
        sinks = []
        for g in range(A_KV_HEADS):
            sk = jnp.full((A_GROUP * chunk, 1), sinks_ref[layer, g * A_GROUP + A_GROUP - 1], f32)
            for j in range(A_GROUP - 2, -1, -1):
                sk = jnp.where(rows < (j + 1) * chunk, sinks_ref[layer, g * A_GROUP + j], sk)
            sinks.append(sk)

        def scores(c):
            q_st = jnp.concatenate([qc[c * chunk:(c + 1) * chunk, :] for qc in q_cols], axis=0)
            out = []
            for k_ref in (klo_s, khi_s):
                s = _dot_nt(q_st, k_ref[c * chunk:c * chunk + win, :])
                if mask_start and c * chunk < keep:
                    first_valid = keep - (t * tile + c * chunk)
                    s = jnp.where(key_col >= first_valid, s, NEG_INF)
                out.append(s)
            return out

        def softmax(s, sk):
            m = jnp.maximum(jnp.max(s, axis=-1, keepdims=True), sk)
            p = jnp.exp(s - m)
            den = jnp.sum(p, axis=-1, keepdims=True) + jnp.exp(sk - m)
            return (p * (1.0 / den)).astype(bf16)

        def values(c, probs):
            o_groups = [_dot(p, v_s[c * chunk:c * chunk + win, :]) for p in probs]
            o_sel = jnp.where(low_half, o_groups[0], o_groups[1])
            return jnp.concatenate([o_sel[j * chunk:(j + 1) * chunk, :] for j in range(A_GROUP)], axis=1)

        oa_chunks = []
        s_next = scores(0)
        yield
        for c in range(n_ch):
            s_cur = s_next
            if c + 1 < n_ch:
                s_next = scores(c + 1)
            probs = [softmax(s, sk) for s, sk in zip(s_cur, sinks)]
            yield
            oa_chunks.append(values(c, probs))
            yield
        o_a = jnp.concatenate(oa_chunks, axis=0).astype(bf16) if n_ch > 1 else oa_chunks[0].astype(bf16)

        if tile >= keep:
            klo_s[0:keep, :] = klo_s[tile:tile + keep, :]
            khi_s[0:keep, :] = khi_s[tile:tile + keep, :]
            v_s[0:keep, :] = v_s[tile:tile + keep, :]
        gated("a", _dot(o_a, wpa_ref[...]))

    def task_m():
        mq = _dot(xb, w_in_ref[:, _C_MQ:_C_GT]).astype(bf16)
        yield
        heads = [slice(h * M_HEAD_DIM, (h + 1) * M_HEAD_DIM) for h in range(M_HEADS)]

        def scores(hs):
            return _dot_nt(mq[:, hs], mk_ref[0, :, hs]) * (M_HEAD_DIM ** -0.5)

        def softmax(s):
            m = jnp.max(s, axis=-1, keepdims=True)
            p = jnp.exp(s - m)
            return (p * (1.0 / jnp.sum(p, axis=-1, keepdims=True))).astype(bf16)

        om_heads = []
        for i in range(0, M_HEADS, 2):
            pair = heads[i:i + 2]
            s_pair = [scores(hs) for hs in pair]
            yield
            p_pair = [softmax(s) for s in s_pair]
            yield
            om_heads += [_dot(p, mv_ref[0, :, hs]) for p, hs in zip(p_pair, pair)]
            yield
        o_m = jnp.concatenate(om_heads, axis=1).astype(bf16)
        gated("m", _dot(o_m, wpm_ref[...]))

    def task_b():
        gk = _dot(res["gk1"], w_gk2_ref[...]) + b_gk_ref[...]
        log_a = (jnp.minimum(gk, 0.0) - jnp.log(1.0 + jnp.exp(-jnp.abs(gk)))) * (1.0 / GATE_TAU)
        yield
        gsub = min(tile, GLA_SUB)
        subs = [slice(s * gsub, (s + 1) * gsub) for s in range(tile // gsub)]
        r_i = lax.broadcasted_iota(jnp.int32, (gsub, gsub), 0)
        c_i = lax.broadcasted_iota(jnp.int32, (gsub, gsub), 1)
        if gsub > chunk:
            causal = (c_i <= r_i) & ((r_i // chunk) == (c_i // chunk))
        else:
            causal = c_i <= r_i
        tri = jnp.where(causal, 1.0, 0.0).astype(bf16)
        la_hi = log_a.astype(bf16)
        la_lo = (log_a - la_hi.astype(f32)).astype(bf16)
        cum = [_dot(tri, la_hi[rs, :]) + _dot(tri, la_lo[rs, :]) for rs in subs]
        cum = jnp.concatenate(cum, axis=0) if len(subs) > 1 else cum[0]
        last_rows = [cum[(c + 1) * chunk - 1:(c + 1) * chunk, :] for c in range(n_ch)]
        if n_ch > 1:
            last = jnp.concatenate([jnp.broadcast_to(r, (chunk, B_QK)) for r in last_rows], axis=0)
        else:
            last = last_rows[0]
        pad_rows = jnp.zeros((LANES - n_ch, B_QK), f32)
        decay_t = jnp.exp(jnp.concatenate(last_rows + [pad_rows], axis=0)).T
        yield
        bq = _dot(xb, w_in_ref[:, _C_BQ:_C_BK])
        q_dec = (bq * (B_DK ** -0.5) * jnp.exp(cum)).astype(bf16)
        yield
        bk = _dot(xb, w_in_ref[:, _C_BK:_C_BV])
        k_inv = (bk * jnp.exp(-cum)).astype(bf16)
        k_end = (bk * jnp.exp(last - cum)).astype(bf16)
        yield
        gn = gn_ref[...]
        ob_heads = []
        for h0 in range(0, B_HEADS, 2):
            pair = (h0, h0 + 1)
            ks = [slice(h * B_DK, (h + 1) * B_DK) for h in pair]
            vv = _dot(xb, w_in_ref[:, _C_BV + h0 * B_DV:_C_BV + (h0 + 2) * B_DV]).astype(bf16)
            vh = [vv[:, :B_DV], vv[:, B_DV:]]
            qd = [q_dec[:, s] for s in ks]
            ke = [k_end[:, s] for s in ks]
            att = [[jnp.where(causal, _dot_nt(qd[i][rs, :], k_inv[rs, ks[i]]), 0.0).astype(bf16) for i in range(2)]
                   for rs in subs]
            yield
            o_intra = [[_dot(att[s][i], vh[i][rs, :]) for i in range(2)] for s, rs in enumerate(subs)]
            o_intra = [jnp.concatenate([o[i] for o in o_intra], axis=0) if len(subs) > 1 else o_intra[0][i]
                       for i in range(2)]
            chunks = [slice(c * chunk, (c + 1) * chunk) for c in range(n_ch)]
            upd = [[_dot_tn(ke[i][cs, :], vh[i][cs, :]) for i in range(2)] for cs in chunks]
            state = [st_s[h] for h in pair]
            o_inter = [[], []]
            for c, cs in enumerate(chunks):
                for i in range(2):
                    o_inter[i].append(_dot(qd[i][cs, :], state[i].astype(bf16)))
                state = [state[i] * decay_t[ks[i], c:c + 1] + upd[c][i] for i in range(2)]
            for i, h in enumerate(pair):
                st_s[h] = state[i]
            yield
            gg = _dot(xb, w_in_ref[:, _C_BG + h0 * B_DV:_C_BG + (h0 + 2) * B_DV])
            for i in range(2):
                o = o_intra[i] + (jnp.concatenate(o_inter[i], axis=0) if n_ch > 1 else o_inter[i][0])
                o = o * lax.rsqrt(jnp.mean(o * o, axis=-1, keepdims=True) + RMS_EPS) * gn
                g_h = gg[:, i * B_DV:(i + 1) * B_DV]
                ob_heads.append((o * (g_h * _sigmoid(g_h))).astype(bf16))
            yield
        o_b = jnp.concatenate(ob_heads, axis=1)
        gated("b", _dot(o_b, wpb_ref[...]))

    tasks = {
        "A": task_a(), "M": task_m(), "B": task_b(),
        "ga": gate_task("a", _C_GT), "gb": gate_task("b", _C_GT + D_MODEL),
        "gm": gate_task("m", _C_GT + 2 * D_MODEL),
    }
    for name in MIXER_ORDER.split():
        next(tasks[name], None)
    for gen in tasks.values():
        for _ in gen:
            pass

    for key in "abm":
        if "gated_" + key not in res:
            res["gated_" + key] = res["gate_" + key] * res["proj_" + key]
    rb = min(tile, OUT_ROWS)
    pending = None
    for r in range(tile // rb):
        rs = slice(r * rb, (r + 1) * rb)
        merged = res["gated_a"][rs] + res["gated_b"][rs] + res["gated_m"][rs]
        z = alpha * x[rs] + _dot(merged.astype(bf16), wout_ref[...])
        if pending is not None:
            y_ref[0, pending[0], :] = _layer_norm(pending[1], ln_g_ref[...], ln_b_ref[...])
        pending = (rs, z)
    y_ref[0, pending[0], :] = _layer_norm(pending[1], ln_g_ref[...], ln_b_ref[...])

    @pl.when(t == last_t)
    def _emit_state_and_window():
        st_ref[0] = st_s[...]

        def emit(ref, lo, hi, rows):
            for g in range(A_KV_HEADS):
                ref[0, lo:hi, g, :] = rows[:, g * A_HEAD_DIM:(g + 1) * A_HEAD_DIM]

        if tile < keep:
            emit(wk_ref, 0, keep - tile, ck_ref[0, tile:keep, :])
            emit(wv_ref, 0, keep - tile, cv_ref[0, tile:keep, :])
        emit(wk_ref, max(keep - tile, 0), keep, res["k_tail"])
        emit(wv_ref, max(keep - tile, 0), keep, res["v_tail"])


def _layer_spec(stacked, layer):
    rest = stacked.shape[1:]
    index = lambda *_: (layer,) + (0,) * len(rest)
    return pl.BlockSpec((None,) + rest, index, pipeline_mode=pl.Buffered(1))


def _mixer(x, rope_tabs, mk, mv, cache, lw, layer, st_prev, *, tile, chunk, alpha):
    nb, seq, _ = x.shape
    n_mem = mk.shape[2]
    keep = WINDOW
    has_cache = cache is not None
    grid = (nb, seq // tile)
    assert seq % tile == 0 and tile % chunk == 0
    assert tile >= keep or (has_cache and grid[1] == 1)

    per_b = lambda b, t: (b, 0, 0)
    layer_b = lambda b, t: (layer, b, 0, 0)
    in_specs = [
        pl.BlockSpec(memory_space=pltpu.SMEM),
        pl.BlockSpec((1, tile, D_MODEL), lambda b, t: (b, t, 0)),
        pl.BlockSpec((tile, LANES), lambda b, t: (t, 0)),
        pl.BlockSpec((tile, LANES), lambda b, t: (t, 0)),
        pl.BlockSpec((tile, LANES), lambda b, t: (t, 0)),
        pl.BlockSpec((None, 1, n_mem, M_Q), layer_b),
        pl.BlockSpec((None, 1, n_mem, M_Q), layer_b),
    ]
    args = [lw["sinks"], x, *rope_tabs, mk, mv]
    if has_cache:
        in_specs += [
            pl.BlockSpec((None, 1, keep, LANES), layer_b),
            pl.BlockSpec((None, 1, keep, LANES), layer_b),
            pl.BlockSpec((None, 1, B_HEADS, B_DK, B_DV), lambda b, t: (layer, b, 0, 0, 0)),
        ]
        args += list(cache)
    weights = [lw["w_in"], lw["w_gk2"], lw["b_gk"], lw["gn"], lw["wpa"], lw["wpb"], lw["wpm"], lw["wout"],
               lw["ln1_g"], lw["ln1_b"]]
    in_specs += [_layer_spec(w, layer) for w in weights]
    args += weights
    aliases = {}
    if st_prev is not None:
        in_specs.append(pl.BlockSpec(memory_space=pl.ANY))
        args.append(st_prev)
        aliases = {len(args) - 1: 3}
    depth = lw["w_in"].shape[0]

    out_shape = (
        jax.ShapeDtypeStruct((nb, seq, D_MODEL), f32),
        jax.ShapeDtypeStruct((nb, keep, A_KV_HEADS, A_HEAD_DIM), f32),
        jax.ShapeDtypeStruct((nb, keep, A_KV_HEADS, A_HEAD_DIM), f32),
        jax.ShapeDtypeStruct((depth, nb, B_HEADS, B_DK, B_DV), f32),
    )
    out_specs = (
        pl.BlockSpec((1, tile, D_MODEL), lambda b, t: (b, t, 0)),
        pl.BlockSpec((1, keep, A_KV_HEADS, A_HEAD_DIM), lambda b, t: (b, 0, 0, 0)),
        pl.BlockSpec((1, keep, A_KV_HEADS, A_HEAD_DIM), lambda b, t: (b, 0, 0, 0)),
        pl.BlockSpec((None, 1, B_HEADS, B_DK, B_DV), lambda b, t: (layer, b, 0, 0, 0)),
    )
    scratch = [
        pltpu.VMEM((keep + tile, LANES), bf16),
        pltpu.VMEM((keep + tile, LANES), bf16),
        pltpu.VMEM((keep + tile, LANES), bf16),
        pltpu.VMEM((B_HEADS, B_DK, B_DV), f32),
    ]
    kern = functools.partial(_mixer_kernel, layer=layer, tile=tile, chunk=chunk, keep=keep, has_cache=has_cache,
                             has_prev=st_prev is not None, mask_start=not has_cache, alpha=alpha)
    return pl.pallas_call(
        kern,
        grid=grid,
        in_specs=in_specs,
        out_specs=out_specs,
        out_shape=out_shape,
        scratch_shapes=scratch,
        input_output_aliases=aliases,
        compiler_params=pltpu.CompilerParams(
            dimension_semantics=("arbitrary", "arbitrary"), vmem_limit_bytes=VMEM_LIMIT_BYTES),
        name="mixer_sample" if has_cache else "mixer_prompt",
    )(*args)


def _mlp_kernel(x_ref, wup_ref, bup_ref, wdn_ref, bdn_ref, g_ref, b_ref, y_ref, *, ff_block, sub, alpha):
    pending = None

    def finish(rows, z):
        y_ref[rows, :] = _layer_norm(z, g_ref[...], b_ref[...])

    for r in range(x_ref.shape[0] // sub):
        rows = slice(r * sub, (r + 1) * sub)
        x = x_ref[rows, :]
        xb = x.astype(bf16)
        acc = None
        for j in range(D_FF // ff_block):
            fs = slice(j * ff_block, (j + 1) * ff_block)
            h = jnp.maximum(_dot(xb, wup_ref[:, fs]) + bup_ref[:, fs], 0.0)
            part = _dot((h * h).astype(bf16), wdn_ref[fs, :])
            acc = part if acc is None else acc + part
            if j == 0 and pending is not None:
                finish(*pending)
        pending = (rows, alpha * x + (acc + bdn_ref[...]))
    finish(*pending)


def _mlp(x2d, lw, layer, *, tile, alpha):
    n = x2d.shape[0]
    assert n % tile == 0
    weights = [lw["wup"], lw["bup"], lw["wdn"], lw["bdn"], lw["ln2_g"], lw["ln2_b"]]
    return pl.pallas_call(
        functools.partial(_mlp_kernel, ff_block=1024, sub=min(tile, MLP_SUB), alpha=alpha),
        grid=(n // tile,),
        in_specs=[pl.BlockSpec((tile, D_MODEL), lambda i: (i, 0))] + [_layer_spec(w, layer) for w in weights],
        out_specs=pl.BlockSpec((tile, D_MODEL), lambda i: (i, 0)),
        out_shape=jax.ShapeDtypeStruct((n, D_MODEL), f32),
        compiler_params=pltpu.CompilerParams(
            dimension_semantics=("arbitrary",), vmem_limit_bytes=VMEM_LIMIT_BYTES),
        name="mlp",
    )(x2d, *weights)


def _memkv_kernel(mem_ref, w_ref, k_ref, v_ref, kb_ref, vb_ref):
    rows, n_mem, _ = mem_ref.shape
    kv = _dot(mem_ref[...].reshape(rows * n_mem, D_MODEL).astype(bf16), w_ref[...])
    k = kv[:, :M_Q]
    v = kv[:, M_Q:]
    for h in range(M_HEADS):
        hs = slice(h * M_HEAD_DIM, (h + 1) * M_HEAD_DIM)
        k_ref[pl.ds(h, rows * n_mem, stride=M_HEADS), :] = k[:, hs]
        v_ref[pl.ds(h, rows * n_mem, stride=M_HEADS), :] = v[:, hs]
    kb_ref[...] = k.reshape(rows, n_mem, M_Q).astype(bf16)
    vb_ref[...] = v.reshape(rows, n_mem, M_Q).astype(bf16)


def _memkv(mem, w_mem_kv_b):
    depth = w_mem_kv_b.shape[0]
    nb, n_mem, _ = mem.shape
    rows = _pick_tile(nb, 4)
    out_spec = pl.BlockSpec((None, rows, n_mem, M_Q), lambda l, b: (l, b, 0, 0))
    return pl.pallas_call(
        _memkv_kernel,
        grid=(depth, nb // rows),
        in_specs=[pl.BlockSpec((rows, n_mem, D_MODEL), lambda l, b: (b, 0, 0)),
                  pl.BlockSpec((None, D_MODEL, 2 * M_Q), lambda l, b: (l, 0, 0))],
        out_specs=(pl.BlockSpec((None, rows * n_mem * M_HEADS, M_HEAD_DIM), lambda l, b: (l, b, 0)),) * 2
        + (out_spec,) * 2,
        out_shape=(jax.ShapeDtypeStruct((depth, nb * n_mem * M_HEADS, M_HEAD_DIM), f32),) * 2
        + (jax.ShapeDtypeStruct((depth, nb, n_mem, M_Q), bf16),) * 2,
        compiler_params=pltpu.CompilerParams(
            dimension_semantics=("arbitrary", "arbitrary"), vmem_limit_bytes=VMEM_LIMIT_BYTES),
        name="memkv",
    )(mem, w_mem_kv_b)


def _rope_tables(pos):
    half = ROT_DIM // 2
    inv = ROPE_THETA ** (-jnp.arange(half, dtype=f32) / half)
    ang = pos.astype(f32)[:, None] * inv[None, :]
    cos, sin = jnp.cos(ang), jnp.sin(ang)
    n = pos.shape[0]
    pad = jnp.zeros((n, A_HEAD_DIM - ROT_DIM), f32)
    zero = jnp.zeros((n, half), f32)
    cos_h = jnp.concatenate([cos, cos, jnp.ones_like(pad)], axis=1)
    lo_h = jnp.concatenate([-sin, zero, pad], axis=1)
    hi_h = jnp.concatenate([zero, sin, pad], axis=1)
    return tuple(jnp.concatenate([t_, t_], axis=1) for t_ in (cos_h, lo_h, hi_h))


_HEAD_ORDER = [h for j in range(A_GROUP) for h in (j, j + A_GROUP)]
_S_BQ = A_Q + 2 * A_KV
_S_GK = _S_BQ + 2 * B_QK + 2 * B_V
_S_MQ = _S_GK + GATE_RANK
_S_END = _S_MQ + M_Q + N_BRANCH * D_MODEL


def _relayout_kernel(src_ref, dst_ref):
    def move(dst_lo, src_lo, width):
        dst_ref[:, dst_lo:dst_lo + width] = src_ref[:, src_lo:src_lo + width].astype(bf16)

    for k, h in enumerate(_HEAD_ORDER):
        move(_C_AQ + k * A_HEAD_DIM, h * A_HEAD_DIM, A_HEAD_DIM)
    move(_C_AK, A_Q, 2 * A_KV)
    dst_ref[:, _C_GK:_C_BQ] = jnp.zeros((dst_ref.shape[0], GK_PAD), bf16)
    move(_C_GK, _S_GK, GATE_RANK)
    move(_C_BQ, _S_BQ, _S_GK - _S_BQ)
    move(_C_MQ, _S_MQ, _S_END - _S_MQ)


def _relayout_w_in(w_in):
    depth, d_model, d_in = w_in.shape
    assert d_in == _S_END
    rows = _pick_tile(d_model, 128)
    return pl.pallas_call(
        _relayout_kernel,
        grid=(depth, d_model // rows),
        in_specs=[pl.BlockSpec((None, rows, d_in), lambda l, r: (l, r, 0))],
        out_specs=pl.BlockSpec((None, rows, _C_END), lambda l, r: (l, r, 0)),
        out_shape=jax.ShapeDtypeStruct((depth, d_model, _C_END), bf16),
        compiler_params=pltpu.CompilerParams(
            dimension_semantics=("arbitrary", "arbitrary"), vmem_limit_bytes=VMEM_LIMIT_BYTES),
        name="relayout_w_in",
    )(w_in)


def _prep_weights(w_in, w_gk2, b_gk, attn_sinks, gla_norm_g, w_proj_a, w_proj_b, w_proj_m, w_out,
                  ln1_g, ln1_b, w_up, b_up, w_down, b_down, ln2_g, ln2_b):
    head_order = _HEAD_ORDER
    w_in_r = _relayout_w_in(w_in)
    wpa = jnp.concatenate([w_proj_a[:, h * A_HEAD_DIM:(h + 1) * A_HEAD_DIM, :] for h in head_order], axis=1)
    row = lambda v: v[:, None, :]
    return dict(
        sinks=attn_sinks,
        w_in=w_in_r,
        w_gk2=jnp.pad(w_gk2, ((0, 0), (0, GK_PAD - GATE_RANK), (0, 0))).astype(bf16),
        b_gk=row(b_gk), gn=row(gla_norm_g),
        wpa=wpa.astype(bf16), wpb=w_proj_b.astype(bf16), wpm=w_proj_m.astype(bf16), wout=w_out.astype(bf16),
        ln1_g=row(ln1_g), ln1_b=row(ln1_b),
        wup=w_up.astype(bf16), bup=row(b_up), wdn=w_down.astype(bf16), bdn=row(b_down),
        ln2_g=row(ln2_g), ln2_b=row(ln2_b),
    )


def _pick_tile(n, pref):
    tile = min(pref, n)
    while n % tile:
        tile //= 2
    return tile


def kernel(x_prompt, x_sample, cache_win_k, cache_win_v, state_gla, cache_mem_k, cache_mem_v, mem_prompt,
           w_in, w_gk2, b_gk, attn_sinks, gla_norm_g, w_mem_kv, w_proj_a, w_proj_b, w_proj_m, w_out,
           ln1_g, ln1_b, w_up, b_up, w_down, b_down, ln2_g, ln2_b):
    depth = w_in.shape[0]
    alpha = (2 * depth) ** 0.25
    lws = _prep_weights(w_in, w_gk2, b_gk, attn_sinks, gla_norm_g, w_proj_a, w_proj_b, w_proj_m, w_out,
                        ln1_g, ln1_b, w_up, b_up, w_down, b_down, ln2_g, ln2_b)

    bp, sp, _ = x_prompt.shape
    n_mem = mem_prompt.shape[1]
    mk_p, mv_p, mk_b, mv_b = _memkv(mem_prompt, w_mem_kv.astype(bf16))
    rope_p = _rope_tables(jnp.arange(sp, dtype=jnp.int32))
    mix_tile = _pick_tile(sp, MIX_TILE)
    mlp_tile = _pick_tile(bp * sp, 1024)
    x = x_prompt
    wk_p, wv_p, gs_p = [], [], None
    for l in range(depth):
        x, wk, wv, gs_p = _mixer(x, rope_p, mk_b, mv_b, None, lws, l, gs_p, tile=mix_tile, chunk=CHUNK, alpha=alpha)
        x = _mlp(x.reshape(bp * sp, D_MODEL), lws, l, tile=mlp_tile, alpha=alpha).reshape(bp, sp, D_MODEL)
        wk_p.append(wk)
        wv_p.append(wv)
    y_prompt = x

    bs, ts, _ = x_sample.shape
    keep = cache_win_k.shape[2]
    rope_s = _rope_tables(PAST_LEN + jnp.arange(ts, dtype=jnp.int32))
    cache = (cache_win_k.reshape(depth, bs, keep, A_KV), cache_win_v.reshape(depth, bs, keep, A_KV), state_gla)
    mk_s = cache_mem_k.reshape(depth, bs, n_mem, M_Q).astype(bf16)
    mv_s = cache_mem_v.reshape(depth, bs, n_mem, M_Q).astype(bf16)
    x = x_sample
    wk_s, wv_s, gs_s = [], [], None
    for l in range(depth):
        x, wk, wv, gs_s = _mixer(x, rope_s, mk_s, mv_s, cache, lws, l, gs_s, tile=ts, chunk=ts, alpha=alpha)
        x = _mlp(x.reshape(bs * ts, D_MODEL), lws, l, tile=_pick_tile(bs * ts, 512), alpha=alpha)
        x = x.reshape(bs, ts, D_MODEL)
        wk_s.append(wk)
        wv_s.append(wv)
    y_sample = x

    mem_shape = (depth, bp, n_mem, M_HEADS, M_HEAD_DIM)
    return (y_prompt, y_sample,
            jnp.stack(wk_p), jnp.stack(wv_p), gs_p,
            mk_p.reshape(mem_shape), mv_p.reshape(mem_shape),
            jnp.stack(wk_s), jnp.stack(wv_s), gs_s)
```

```python
import functools

import jax
import jax.numpy as jnp
from jax import lax
from jax.experimental import pallas as pl
from jax.experimental.pallas import tpu as pltpu

D_MODEL = 1024
PAST_LEN = 1024
CHUNK = 64
WINDOW = 128
A_HEADS = 8
A_KV_HEADS = 2
A_HEAD_DIM = 64
A_GROUP = A_HEADS // A_KV_HEADS
A_SCALE = A_HEAD_DIM ** -0.5
ROT_DIM = A_HEAD_DIM // 4
ROPE_THETA = 500000.0
A_Q = A_HEADS * A_HEAD_DIM
A_KV = A_KV_HEADS * A_HEAD_DIM
B_HEADS = 4
B_DK = 128
B_DV = 256
B_QK = B_HEADS * B_DK
B_V = B_HEADS * B_DV
GATE_RANK = 16
GATE_TAU = 16.0
M_HEADS = 4
M_HEAD_DIM = 128
M_Q = M_HEADS * M_HEAD_DIM
D_FF = 4 * D_MODEL
N_BRANCH = 3
LN_EPS = 1e-5
RMS_EPS = 1e-6
NEG_INF = -1e30

LANES = 128
GK_PAD = LANES
VMEM_LIMIT_BYTES = 60000 * 1024
GATE_BLOCK = 512
MLP_SUB = 256
GLA_SUB = 256
MIX_TILE = 512
OUT_ROWS = 256
MIXER_ORDER = ("A M B A ga B A M B A ga A A M A B M A gm A M A B A M A gm A M A B A gb A B A M A B A gb"
               " B A B B")

_C_AQ = 0
_C_AK = _C_AQ + A_Q
_C_AV = _C_AK + A_KV
_C_GK = _C_AV + A_KV
_C_BQ = _C_GK + GK_PAD
_C_BK = _C_BQ + B_QK
_C_BV = _C_BK + B_QK
_C_BG = _C_BV + B_V
_C_MQ = _C_BG + B_V
_C_GT = _C_MQ + M_Q
_C_END = _C_GT + N_BRANCH * D_MODEL

_NT = (((1,), (1,)), ((), ()))
_TN = (((0,), (0,)), ((), ()))

bf16 = jnp.bfloat16
f32 = jnp.float32


def _dot(a, b):
    return jnp.dot(a, b, preferred_element_type=f32)


def _dot_nt(a, b):
    return lax.dot_general(a, b, _NT, preferred_element_type=f32)


def _dot_tn(a, b):
    return lax.dot_general(a, b, _TN, preferred_element_type=f32)


def _layer_norm(z, g, b):
    mu = jnp.mean(z, axis=-1, keepdims=True)
    zc = z - mu
    var = jnp.mean(zc * zc, axis=-1, keepdims=True)
    return zc * lax.rsqrt(var + LN_EPS) * g + b


def _sigmoid(z):
    return 0.5 * jnp.tanh(0.5 * z) + 0.5


def _mixer_kernel(*refs, layer, tile, chunk, keep, has_cache, has_prev, mask_start, alpha):
    it = iter(refs)
    sinks_ref = next(it)
    x_ref = next(it)
    cos_ref, sin_lo_ref, sin_hi_ref = next(it), next(it), next(it)
    mk_ref, mv_ref = next(it), next(it)
    if has_cache:
        ck_ref, cv_ref, s0_ref = next(it), next(it), next(it)
    w_in_ref, w_gk2_ref, b_gk_ref, gn_ref = next(it), next(it), next(it), next(it)
    wpa_ref, wpb_ref, wpm_ref, wout_ref = next(it), next(it), next(it), next(it)
    ln_g_ref, ln_b_ref = next(it), next(it)
    if has_prev:
        next(it)
    y_ref, wk_ref, wv_ref, st_ref = next(it), next(it), next(it), next(it)
    klo_s, khi_s, v_s, st_s = next(it), next(it), next(it), next(it)

    t = pl.program_id(1)
    last_t = pl.num_programs(1) - 1
    n_ch = tile // chunk
    win = keep + chunk

    lane = lax.broadcasted_iota(jnp.int32, (1, LANES), 1)
    low_half = lane < A_HEAD_DIM

    @pl.when(t == 0)
    def _init():
        if has_cache:
            ck = ck_ref[0]
            klo_s[0:keep, :] = jnp.where(low_half, ck, 0.0).astype(bf16)
            khi_s[0:keep, :] = jnp.where(low_half, 0.0, ck).astype(bf16)
            v_s[0:keep, :] = cv_ref[0].astype(bf16)
            for h in range(B_HEADS):
                st_s[h] = s0_ref[0, h]
        else:
            zero = jnp.zeros((keep, LANES), bf16)
            klo_s[0:keep, :] = zero
            khi_s[0:keep, :] = zero
            v_s[0:keep, :] = zero
            st_s[...] = jnp.zeros(st_s.shape, f32)

    x = x_ref[0]
    xb = x.astype(bf16)
    res = {}

    def gated(key, proj):
        if "gate_" + key in res:
            res["gated_" + key] = res.pop("gate_" + key) * proj
        else:
            res["proj_" + key] = proj

    def gate_task(key, col):
        blocks = []
        n_blocks = D_MODEL // GATE_BLOCK
        for j in range(n_blocks):
            cs = slice(col + j * GATE_BLOCK, col + (j + 1) * GATE_BLOCK)
            blocks.append(_sigmoid(_dot(xb, w_in_ref[:, cs])))
            if j == n_blocks - 1:
                gate = jnp.concatenate(blocks, axis=1) if n_blocks > 1 else blocks[0]
                if "proj_" + key in res:
                    res["gated_" + key] = gate * res.pop("proj_" + key)
                else:
                    res["gate_" + key] = gate
            yield

    def task_a():
        cos = cos_ref[...]
        sin_lo = sin_lo_ref[...]
        sin_hi = sin_hi_ref[...]

        def rope(z):
            return (z * cos + pltpu.roll(z, LANES - ROT_DIM // 2, 1) * sin_lo
                    + pltpu.roll(z, ROT_DIM // 2, 1) * sin_hi)

        qkv = _dot(xb, w_in_ref[:, _C_AQ:_C_BQ])
        res["gk1"] = qkv[:, _C_GK:_C_BQ].astype(bf16)
        yield
        q_cols = [(rope(qkv[:, j * LANES:(j + 1) * LANES]) * A_SCALE).astype(bf16) for j in range(A_GROUP)]
        k_r = rope(qkv[:, _C_AK:_C_AV])
        v_new = qkv[:, _C_AV:_C_GK]
        klo_s[keep:keep + tile, :] = jnp.where(low_half, k_r, 0.0).astype(bf16)
        khi_s[keep:keep + tile, :] = jnp.where(low_half, 0.0, k_r).astype(bf16)
        v_s[keep:keep + tile, :] = v_new.astype(bf16)

        res["k_tail"] = k_r[max(tile - keep, 0):, :]
        res["v_tail"] = v_new[max(tile - keep, 0):, :]
        yield

        rows = lax.broadcasted_iota(jnp.int32, (A_GROUP * chunk, 1), 0)
        key_col = lax.broadcasted_iota(jnp.int32, (1, win), 1)
        sinks = []
        for g in range(A_KV_HEADS):
            sk = jnp.full((A_GROUP * chunk, 1), sinks_ref[layer, g * A_GROUP + A_GROUP - 1], f32)
            for j in range(A_GROUP - 2, -1, -1):
                sk = jnp.where(rows < (j + 1) * chunk, sinks_ref[layer, g * A_GROUP + j], sk)
            sinks.append(sk)

        def scores(c):
            q_st = jnp.concatenate([qc[c * chunk:(c + 1) * chunk, :] for qc in q_cols], axis=0)
            out = []
            for k_ref in (klo_s, khi_s):
                s = _dot_nt(q_st, k_ref[c * chunk:c * chunk + win, :])
                if mask_start and c * chunk < keep:
                    first_valid = keep - (t * tile + c * chunk)
                    s = jnp.where(key_col >= first_valid, s, NEG_INF)
                out.append(s)
            return out

        def softmax(s, sk):
            m = jnp.maximum(jnp.max(s, axis=-1, keepdims=True), sk)
            p = jnp.exp(s - m)
            den = jnp.sum(p, axis=-1, keepdims=True) + jnp.exp(sk - m)
            return (p * (1.0 / den)).astype(bf16)

        def values(c, probs):
            o_groups = [_dot(p, v_s[c * chunk:c * chunk + win, :]) for p in probs]
            o_sel = jnp.where(low_half, o_groups[0], o_groups[1])
            return jnp.concatenate([o_sel[j * chunk:(j + 1) * chunk, :] for j in range(A_GROUP)], axis=1)

        oa_chunks = []
        s_next = scores(0)
        yield
        for c in range(n_ch):
            s_cur = s_next
            if c + 1 < n_ch:
                s_next = scores(c + 1)
            probs = [softmax(s, sk) for s, sk in zip(s_cur, sinks)]
            yield
            oa_chunks.append(values(c, probs))
            yield
        o_a = jnp.concatenate(oa_chunks, axis=0).astype(bf16) if n_ch > 1 else oa_chunks[0].astype(bf16)

        if tile >= keep:
            klo_s[0:keep, :] = klo_s[tile:tile + keep, :]
            khi_s[0:keep, :] = khi_s[tile:tile + keep, :]
            v_s[0:keep, :] = v_s[tile:tile + keep, :]
        gated("a", _dot(o_a, wpa_ref[...]))

    def task_m():
        mq = _dot(xb, w_in_ref[:, _C_MQ:_C_GT]).astype(bf16)
        yield
        heads = [slice(h * M_HEAD_DIM, (h + 1) * M_HEAD_DIM) for h in range(M_HEADS)]

        def scores(hs):
            return _dot_nt(mq[:, hs], mk_ref[0, :, hs]) * (M_HEAD_DIM ** -0.5)

        def softmax(s):
            m = jnp.max(s, axis=-1, keepdims=True)
            p = jnp.exp(s - m)
            return (p * (1.0 / jnp.sum(p, axis=-1, keepdims=True))).astype(bf16)

        om_heads = []
        for i in range(0, M_HEADS, 2):
            pair = heads[i:i + 2]
            s_pair = [scores(hs) for hs in pair]
            yield
            p_pair = [softmax(s) for s in s_pair]
            yield
            om_heads += [_dot(p, mv_ref[0, :, hs]) for p, hs in zip(p_pair, pair)]
            yield
        o_m = jnp.concatenate(om_heads, axis=1).astype(bf16)
        gated("m", _dot(o_m, wpm_ref[...]))

    def task_b():
        gk = _dot(res["gk1"], w_gk2_ref[...]) + b_gk_ref[...]
        log_a = (jnp.minimum(gk, 0.0) - jnp.log(1.0 + jnp.exp(-jnp.abs(gk)))) * (1.0 / GATE_TAU)
        yield
        gsub = min(tile, GLA_SUB)
        subs = [slice(s * gsub, (s + 1) * gsub) for s in range(tile // gsub)]
        r_i = lax.broadcasted_iota(jnp.int32, (gsub, gsub), 0)
        c_i = lax.broadcasted_iota(jnp.int32, (gsub, gsub), 1)
        if gsub > chunk:
            causal = (c_i <= r_i) & ((r_i // chunk) == (c_i // chunk))
        else:
            causal = c_i <= r_i
        tri = jnp.where(causal, 1.0, 0.0).astype(bf16)
        la_hi = log_a.astype(bf16)
        la_lo = (log_a - la_hi.astype(f32)).astype(bf16)
        cum = [_dot(tri, la_hi[rs, :]) + _dot(tri, la_lo[rs, :]) for rs in subs]
        cum = jnp.concatenate(cum, axis=0) if len(subs) > 1 else cum[0]
        last_rows = [cum[(c + 1) * chunk - 1:(c + 1) * chunk, :] for c in range(n_ch)]
        if n_ch > 1:
            last = jnp.concatenate([jnp.broadcast_to(r, (chunk, B_QK)) for r in last_rows], axis=0)
        else:
            last = last_rows[0]
        pad_rows = jnp.zeros((LANES - n_ch, B_QK), f32)
        decay_t = jnp.exp(jnp.concatenate(last_rows + [pad_rows], axis=0)).T
        yield
        bq = _dot(xb, w_in_ref[:, _C_BQ:_C_BK])
        q_dec = (bq * (B_DK ** -0.5) * jnp.exp(cum)).astype(bf16)
        yield
        bk = _dot(xb, w_in_ref[:, _C_BK:_C_BV])
        k_inv = (bk * jnp.exp(-cum)).astype(bf16)
        k_end = (bk * jnp.exp(last - cum)).astype(bf16)
        yield
        gn = gn_ref[...]
        ob_heads = []
        for h0 in range(0, B_HEADS, 2):
            pair = (h0, h0 + 1)
            ks = [slice(h * B_DK, (h + 1) * B_DK) for h in pair]
            vv = _dot(xb, w_in_ref[:, _C_BV + h0 * B_DV:_C_BV + (h0 + 2) * B_DV]).astype(bf16)
            vh = [vv[:, :B_DV], vv[:, B_DV:]]
            qd = [q_dec[:, s] for s in ks]
            ke = [k_end[:, s] for s in ks]
            att = [[jnp.where(causal, _dot_nt(qd[i][rs, :], k_inv[rs, ks[i]]), 0.0).astype(bf16) for i in range(2)]
                   for rs in subs]
            yield
            o_intra = [[_dot(att[s][i], vh[i][rs, :]) for i in range(2)] for s, rs in enumerate(subs)]
            o_intra = [jnp.concatenate([o[i] for o in o_intra], axis=0) if len(subs) > 1 else o_intra[0][i]
                       for i in range(2)]
            chunks = [slice(c * chunk, (c + 1) * chunk) for c in range(n_ch)]
            upd = [[_dot_tn(ke[i][cs, :], vh[i][cs, :]) for i in range(2)] for cs in chunks]
            state = [st_s[h] for h in pair]
            o_inter = [[], []]
            for c, cs in enumerate(chunks):
                for i in range(2):
                    o_inter[i].append(_dot(qd[i][cs, :], state[i].astype(bf16)))
                state = [state[i] * decay_t[ks[i], c:c + 1] + upd[c][i] for i in range(2)]
            for i, h in enumerate(pair):
                st_s[h] = state[i]
            yield
            gg = _dot(xb, w_in_ref[:, _C_BG + h0 * B_DV:_C_BG + (h0 + 2) * B_DV])
            for i in range(2):
                o = o_intra[i] + (jnp.concatenate(o_inter[i], axis=0) if n_ch > 1 else o_inter[i][0])
                o = o * lax.rsqrt(jnp.mean(o * o, axis=-1, keepdims=True) + RMS_EPS) * gn
                g_h = gg[:, i * B_DV:(i + 1) * B_DV]
                ob_heads.append((o * (g_h * _sigmoid(g_h))).astype(bf16))
            yield
        o_b = jnp.concatenate(ob_heads, axis=1)
        gated("b", _dot(o_b, wpb_ref[...]))

    tasks = {
        "A": task_a(), "M": task_m(), "B": task_b(),
        "ga": gate_task("a", _C_GT), "gb": gate_task("b", _C_GT + D_MODEL),
        "gm": gate_task("m", _C_GT + 2 * D_MODEL),
    }
    for name in MIXER_ORDER.split():
        next(tasks[name], None)
    for gen in tasks.values():
        for _ in gen:
            pass

    for key in "abm":
        if "gated_" + key not in res:
            res["gated_" + key] = res["gate_" + key] * res["proj_" + key]
    rb = min(tile, OUT_ROWS)
    pending = None
    for r in range(tile // rb):
        rs = slice(r * rb, (r + 1) * rb)
        merged = res["gated_a"][rs] + res["gated_b"][rs] + res["gated_m"][rs]
        z = alpha * x[rs] + _dot(merged.astype(bf16), wout_ref[...])
        if pending is not None:
            y_ref[0, pending[0], :] = _layer_norm(pending[1], ln_g_ref[...], ln_b_ref[...])
        pending = (rs, z)
    y_ref[0, pending[0], :] = _layer_norm(pending[1], ln_g_ref[...], ln_b_ref[...])

    @pl.when(t == last_t)
    def _emit_state_and_window():
        st_ref[0] = st_s[...]

        def emit(ref, lo, hi, rows):
            for g in range(A_KV_HEADS):
                ref[0, lo:hi, g, :] = rows[:, g * A_HEAD_DIM:(g + 1) * A_HEAD_DIM]

        if tile < keep:
            emit(wk_ref, 0, keep - tile, ck_ref[0, tile:keep, :])
            emit(wv_ref, 0, keep - tile, cv_ref[0, tile:keep, :])
        emit(wk_ref, max(keep - tile, 0), keep, res["k_tail"])
        emit(wv_ref, max(keep - tile, 0), keep, res["v_tail"])


def _layer_spec(stacked, layer):
    rest = stacked.shape[1:]
    index = lambda *_: (layer,) + (0,) * len(rest)
    return pl.BlockSpec((None,) + rest, index, pipeline_mode=pl.Buffered(1))


def _mixer(x, rope_tabs, mk, mv, cache, lw, layer, st_prev, *, tile, chunk, alpha):
    nb, seq, _ = x.shape
    n_mem = mk.shape[2]
    keep = WINDOW
    has_cache = cache is not None
    grid = (nb, seq // tile)
    assert seq % tile == 0 and tile % chunk == 0
    assert tile >= keep or (has_cache and grid[1] == 1)

    per_b = lambda b, t: (b, 0, 0)
    layer_b = lambda b, t: (layer, b, 0, 0)
    in_specs = [
        pl.BlockSpec(memory_space=pltpu.SMEM),
        pl.BlockSpec((1, tile, D_MODEL), lambda b, t: (b, t, 0)),
        pl.BlockSpec((tile, LANES), lambda b, t: (t, 0)),
        pl.BlockSpec((tile, LANES), lambda b, t: (t, 0)),
        pl.BlockSpec((tile, LANES), lambda b, t: (t, 0)),
        pl.BlockSpec((None, 1, n_mem, M_Q), layer_b),
        pl.BlockSpec((None, 1, n_mem, M_Q), layer_b),
    ]
    args = [lw["sinks"], x, *rope_tabs, mk, mv]
    if has_cache:
        in_specs += [
            pl.BlockSpec((None, 1, keep, LANES), layer_b),
            pl.BlockSpec((None, 1, keep, LANES), layer_b),
            pl.BlockSpec((None, 1, B_HEADS, B_DK, B_DV), lambda b, t: (layer, b, 0, 0, 0)),
        ]
        args += list(cache)
    weights = [lw["w_in"], lw["w_gk2"], lw["b_gk"], lw["gn"], lw["wpa"], lw["wpb"], lw["wpm"], lw["wout"],
               lw["ln1_g"], lw["ln1_b"]]
    in_specs += [_layer_spec(w, layer) for w in weights]
    args += weights
    aliases = {}
    if st_prev is not None:
        in_specs.append(pl.BlockSpec(memory_space=pl.ANY))
        args.append(st_prev)
        aliases = {len(args) - 1: 3}
    depth = lw["w_in"].shape[0]

    out_shape = (
        jax.ShapeDtypeStruct((nb, seq, D_MODEL), f32),
        jax.ShapeDtypeStruct((nb, keep, A_KV_HEADS, A_HEAD_DIM), f32),
        jax.ShapeDtypeStruct((nb, keep, A_KV_HEADS, A_HEAD_DIM), f32),
        jax.ShapeDtypeStruct((depth, nb, B_HEADS, B_DK, B_DV), f32),
    )
    out_specs = (
        pl.BlockSpec((1, tile, D_MODEL), lambda b, t: (b, t, 0)),
        pl.BlockSpec((1, keep, A_KV_HEADS, A_HEAD_DIM), lambda b, t: (b, 0, 0, 0)),
        pl.BlockSpec((1, keep, A_KV_HEADS, A_HEAD_DIM), lambda b, t: (b, 0, 0, 0)),
        pl.BlockSpec((None, 1, B_HEADS, B_DK, B_DV), lambda b, t: (layer, b, 0, 0, 0)),
    )
    scratch = [
        pltpu.VMEM((keep + tile, LANES), bf16),
        pltpu.VMEM((keep + tile, LANES), bf16),
        pltpu.VMEM((keep + tile, LANES), bf16),
        pltpu.VMEM((B_HEADS, B_DK, B_DV), f32),
    ]
    kern = functools.partial(_mixer_kernel, layer=layer, tile=tile, chunk=chunk, keep=keep, has_cache=has_cache,
                             has_prev=st_prev is not None, mask_start=not has_cache, alpha=alpha)
    return pl.pallas_call(
        kern,
        grid=grid,
        in_specs=in_specs,
        out_specs=out_specs,
        out_shape=out_shape,
        scratch_shapes=scratch,
        input_output_aliases=aliases,
        compiler_params=pltpu.CompilerParams(
            dimension_semantics=("arbitrary", "arbitrary"), vmem_limit_bytes=VMEM_LIMIT_BYTES),
        name="mixer_sample" if has_cache else "mixer_prompt",
    )(*args)


def _mlp_kernel(x_ref, wup_ref, bup_ref, wdn_ref, bdn_ref, g_ref, b_ref, y_ref, z_s, *, ff_block, sub, alpha):
    i = pl.program_id(0)
    n_tiles = pl.num_programs(0) - 1
    subs = [slice(r * sub, (r + 1) * sub) for r in range(x_ref.shape[0] // sub)]

    def finish(rows):
        y_ref[rows, :] = _layer_norm(z_s[rows, :], g_ref[...], b_ref[...])

    @pl.when(i == 0)
    def _init():
        z_s[...] = jnp.zeros(z_s.shape, f32)

    @pl.when(i < n_tiles)
    def _body():
        for rows in subs:
            x = x_ref[rows, :]
            xb = x.astype(bf16)
            acc = None
            for j in range(D_FF // ff_block):
                fs = slice(j * ff_block, (j + 1) * ff_block)
                h = jnp.maximum(_dot(xb, wup_ref[:, fs]) + bup_ref[:, fs], 0.0)
                part = _dot((h * h).astype(bf16), wdn_ref[fs, :])
                acc = part if acc is None else acc + part
                if j == 0:
                    finish(rows)
            z_s[rows, :] = alpha * x + (acc + bdn_ref[...])

    @pl.when(i == n_tiles)
    def _drain():
        for rows in subs:
            finish(rows)


def _mlp(x2d, lw, layer, *, tile, alpha):
    n = x2d.shape[0]
    assert n % tile == 0
    n_tiles = n // tile
    weights = [lw["wup"], lw["bup"], lw["wdn"], lw["bdn"], lw["ln2_g"], lw["ln2_b"]]
    return pl.pallas_call(
        functools.partial(_mlp_kernel, ff_block=1024, sub=min(tile, MLP_SUB), alpha=alpha),
        grid=(n_tiles + 1,),
        in_specs=[pl.BlockSpec((tile, D_MODEL), lambda i: (jnp.minimum(i, n_tiles - 1), 0))]
        + [_layer_spec(w, layer) for w in weights],
        out_specs=pl.BlockSpec((tile, D_MODEL), lambda i: (jnp.maximum(i - 1, 0), 0)),
        out_shape=jax.ShapeDtypeStruct((n, D_MODEL), f32),
        scratch_shapes=[pltpu.VMEM((tile, D_MODEL), f32)],
        compiler_params=pltpu.CompilerParams(
            dimension_semantics=("arbitrary",), vmem_limit_bytes=VMEM_LIMIT_BYTES),
        name="mlp",
    )(x2d, *weights)


def _memkv_kernel(mem_ref, w_ref, k_ref, v_ref, kb_ref, vb_ref):
    rows, n_mem, _ = mem_ref.shape
    kv = _dot(mem_ref[...].reshape(rows * n_mem, D_MODEL).astype(bf16), w_ref[...])
    k = kv[:, :M_Q]
    v = kv[:, M_Q:]
    for h in range(M_HEADS):
        hs = slice(h * M_HEAD_DIM, (h + 1) * M_HEAD_DIM)
        k_ref[pl.ds(h, rows * n_mem, stride=M_HEADS), :] = k[:, hs]
        v_ref[pl.ds(h, rows * n_mem, stride=M_HEADS), :] = v[:, hs]
    kb_ref[...] = k.reshape(rows, n_mem, M_Q).astype(bf16)
    vb_ref[...] = v.reshape(rows, n_mem, M_Q).astype(bf16)


def _memkv(mem, w_mem_kv_b):
    depth = w_mem_kv_b.shape[0]
    nb, n_mem, _ = mem.shape
    rows = _pick_tile(nb, 4)
    out_spec = pl.BlockSpec((None, rows, n_mem, M_Q), lambda l, b: (l, b, 0, 0))
    return pl.pallas_call(
        _memkv_kernel,
        grid=(depth, nb // rows),
        in_specs=[pl.BlockSpec((rows, n_mem, D_MODEL), lambda l, b: (b, 0, 0)),
                  pl.BlockSpec((None, D_MODEL, 2 * M_Q), lambda l, b: (l, 0, 0))],
        out_specs=(pl.BlockSpec((None, rows * n_mem * M_HEADS, M_HEAD_DIM), lambda l, b: (l, b, 0)),) * 2
        + (out_spec,) * 2,
        out_shape=(jax.ShapeDtypeStruct((depth, nb * n_mem * M_HEADS, M_HEAD_DIM), f32),) * 2
        + (jax.ShapeDtypeStruct((depth, nb, n_mem, M_Q), bf16),) * 2,
        compiler_params=pltpu.CompilerParams(
            dimension_semantics=("arbitrary", "arbitrary"), vmem_limit_bytes=VMEM_LIMIT_BYTES),
        name="memkv",
    )(mem, w_mem_kv_b)


def _rope_tables(pos):
    half = ROT_DIM // 2
    inv = ROPE_THETA ** (-jnp.arange(half, dtype=f32) / half)
    ang = pos.astype(f32)[:, None] * inv[None, :]
    cos, sin = jnp.cos(ang), jnp.sin(ang)
    n = pos.shape[0]
    pad = jnp.zeros((n, A_HEAD_DIM - ROT_DIM), f32)
    zero = jnp.zeros((n, half), f32)
    cos_h = jnp.concatenate([cos, cos, jnp.ones_like(pad)], axis=1)
    lo_h = jnp.concatenate([-sin, zero, pad], axis=1)
    hi_h = jnp.concatenate([zero, sin, pad], axis=1)
    return tuple(jnp.concatenate([t_, t_], axis=1) for t_ in (cos_h, lo_h, hi_h))


_HEAD_ORDER = [h for j in range(A_GROUP) for h in (j, j + A_GROUP)]
_S_BQ = A_Q + 2 * A_KV
_S_GK = _S_BQ + 2 * B_QK + 2 * B_V
_S_MQ = _S_GK + GATE_RANK
_S_END = _S_MQ + M_Q + N_BRANCH * D_MODEL


def _relayout_kernel(src_ref, dst_ref):
    def move(dst_lo, src_lo, width):
        dst_ref[:, dst_lo:dst_lo + width] = src_ref[:, src_lo:src_lo + width].astype(bf16)

    for k, h in enumerate(_HEAD_ORDER):
        move(_C_AQ + k * A_HEAD_DIM, h * A_HEAD_DIM, A_HEAD_DIM)
    move(_C_AK, A_Q, 2 * A_KV)
    dst_ref[:, _C_GK:_C_BQ] = jnp.zeros((dst_ref.shape[0], GK_PAD), bf16)
    move(_C_GK, _S_GK, GATE_RANK)
    move(_C_BQ, _S_BQ, _S_GK - _S_BQ)
    move(_C_MQ, _S_MQ, _S_END - _S_MQ)


def _relayout_w_in(w_in):
    depth, d_model, d_in = w_in.shape
    assert d_in == _S_END
    rows = _pick_tile(d_model, 128)
    return pl.pallas_call(
        _relayout_kernel,
        grid=(depth, d_model // rows),
        in_specs=[pl.BlockSpec((None, rows, d_in), lambda l, r: (l, r, 0))],
        out_specs=pl.BlockSpec((None, rows, _C_END), lambda l, r: (l, r, 0)),
        out_shape=jax.ShapeDtypeStruct((depth, d_model, _C_END), bf16),
        compiler_params=pltpu.CompilerParams(
            dimension_semantics=("arbitrary", "arbitrary"), vmem_limit_bytes=VMEM_LIMIT_BYTES),
        name="relayout_w_in",
    )(w_in)


def _prep_weights(w_in, w_gk2, b_gk, attn_sinks, gla_norm_g, w_proj_a, w_proj_b, w_proj_m, w_out,
                  ln1_g, ln1_b, w_up, b_up, w_down, b_down, ln2_g, ln2_b):
    head_order = _HEAD_ORDER
    w_in_r = _relayout_w_in(w_in)
    wpa = jnp.concatenate([w_proj_a[:, h * A_HEAD_DIM:(h + 1) * A_HEAD_DIM, :] for h in head_order], axis=1)
    row = lambda v: v[:, None, :]
    return dict(
        sinks=attn_sinks,
        w_in=w_in_r,
        w_gk2=jnp.pad(w_gk2, ((0, 0), (0, GK_PAD - GATE_RANK), (0, 0))).astype(bf16),
        b_gk=row(b_gk), gn=row(gla_norm_g),
        wpa=wpa.astype(bf16), wpb=w_proj_b.astype(bf16), wpm=w_proj_m.astype(bf16), wout=w_out.astype(bf16),
        ln1_g=row(ln1_g), ln1_b=row(ln1_b),
        wup=w_up.astype(bf16), bup=row(b_up), wdn=w_down.astype(bf16), bdn=row(b_down),
        ln2_g=row(ln2_g), ln2_b=row(ln2_b),
    )


def _pick_tile(n, pref):
    tile = min(pref, n)
    while n % tile:
        tile //= 2
    return tile


def kernel(x_prompt, x_sample, cache_win_k, cache_win_v, state_gla, cache_mem_k, cache_mem_v, mem_prompt,
           w_in, w_gk2, b_gk, attn_sinks, gla_norm_g, w_mem_kv, w_proj_a, w_proj_b, w_proj_m, w_out,
           ln1_g, ln1_b, w_up, b_up, w_down, b_down, ln2_g, ln2_b):
    depth = w_in.shape[0]
    alpha = (2 * depth) ** 0.25
    lws = _prep_weights(w_in, w_gk2, b_gk, attn_sinks, gla_norm_g, w_proj_a, w_proj_b, w_proj_m, w_out,
                        ln1_g, ln1_b, w_up, b_up, w_down, b_down, ln2_g, ln2_b)

    bp, sp, _ = x_prompt.shape
    n_mem = mem_prompt.shape[1]
    mk_p, mv_p, mk_b, mv_b = _memkv(mem_prompt, w_mem_kv.astype(bf16))
    rope_p = _rope_tables(jnp.arange(sp, dtype=jnp.int32))
    mix_tile = _pick_tile(sp, MIX_TILE)
    mlp_tile = _pick_tile(bp * sp, 1024)
    x = x_prompt
    wk_p, wv_p, gs_p = [], [], None
    for l in range(depth):
        x, wk, wv, gs_p = _mixer(x, rope_p, mk_b, mv_b, None, lws, l, gs_p, tile=mix_tile, chunk=CHUNK, alpha=alpha)
        x = _mlp(x.reshape(bp * sp, D_MODEL), lws, l, tile=mlp_tile, alpha=alpha).reshape(bp, sp, D_MODEL)
        wk_p.append(wk)
        wv_p.append(wv)
    y_prompt = x

    bs, ts, _ = x_sample.shape
    keep = cache_win_k.shape[2]
    rope_s = _rope_tables(PAST_LEN + jnp.arange(ts, dtype=jnp.int32))
    cache = (cache_win_k.reshape(depth, bs, keep, A_KV), cache_win_v.reshape(depth, bs, keep, A_KV), state_gla)
    mk_s = cache_mem_k.reshape(depth, bs, n_mem, M_Q).astype(bf16)
    mv_s = cache_mem_v.reshape(depth, bs, n_mem, M_Q).astype(bf16)
    x = x_sample
    wk_s, wv_s, gs_s = [], [], None
    for l in range(depth):
        x, wk, wv, gs_s = _mixer(x, rope_s, mk_s, mv_s, cache, lws, l, gs_s, tile=ts, chunk=ts, alpha=alpha)
        x = _mlp(x.reshape(bs * ts, D_MODEL), lws, l, tile=_pick_tile(bs * ts, 512), alpha=alpha)
        x = x.reshape(bs, ts, D_MODEL)
        wk_s.append(wk)
        wv_s.append(wv)
    y_sample = x

    mem_shape = (depth, bp, n_mem, M_HEADS, M_HEAD_DIM)
    return (y_prompt, y_sample,
            jnp.stack(wk_p), jnp.stack(wv_p), gs_p,
            mk_p.reshape(mem_shape), mv_p.reshape(mem_shape),
            jnp.stack(wk_s), jnp.stack(wv_s), gs_s)
```
